```python
import functools
import jax
import jax.numpy as jnp
from jax import lax
import numpy as np

D_MODEL = 2048
BATCH = 1
SEQ = 8192
DEPTH = 2
DEC_BATCH = 128
DEC_SEQ = 4
PAST_LEN = 8192
PAGE_SIZE = 128

HEAD_DIM = 128
MIX_WIDTH = D_MODEL
SB_HEADS = MIX_WIDTH // 2 // HEAD_DIM
SB_WIDTH = SB_HEADS * HEAD_DIM
SB_BLOCK = 128
SB_BIAS_INIT = -8.0
LRU_WIDTH = MIX_WIDTH // 2
LRU_BLOCKS = 8
LRU_BLOCK = LRU_WIDTH // LRU_BLOCKS
CONV_W = 4
RG_C = 8.0
RET_HEADS = 4
RET_DK = MIX_WIDTH // 2 // RET_HEADS
RET_DV = RET_DK
RET_WIDTH = RET_HEADS * RET_DK
RET_CHUNK = 128
SWA_HEADS = MIX_WIDTH // 2 // HEAD_DIM
SWA_KV_HEADS = 2
SWA_GROUP = SWA_HEADS // SWA_KV_HEADS
SWA_WIDTH = SWA_HEADS * HEAD_DIM
SWA_KV_WIDTH = SWA_KV_HEADS * HEAD_DIM
WINDOW = 128
SWA_BLOCK = 128
N_MEM = 256
MEM_HEADS = 4
MEM_WIDTH = MEM_HEADS * HEAD_DIM
D_FF = 5632
ROPE_THETA = 10000.0
EPS = 1e-6
AB_IN = 3 * SB_WIDTH + 2 * LRU_WIDTH
AB_OUT = SB_WIDTH + LRU_WIDTH
CD_IN = 4 * RET_WIDTH + SWA_WIDTH + 2 * SWA_KV_WIDTH
CD_OUT = RET_WIDTH + SWA_WIDTH
F32 = jnp.float32

kernel_name = 'hybrid_stickbreak_rglru_retention_swa_step'


def rmsnorm(x, g):
    xf = x.astype(F32)
    y = xf * lax.rsqrt(jnp.mean(xf * xf, axis=-1, keepdims=True) + EPS)
    return (y * g.astype(F32)).astype(x.dtype)


def swiglu(h, wg, wu, wd):
    return (jax.nn.silu(h @ wg) * (h @ wu)) @ wd


def split_cols(t, sizes):
    return jnp.split(t, np.cumsum(sizes)[:-1].tolist(), axis=-1)


def rope(x, pos):
    half = x.shape[-1] // 2
    inv = ROPE_THETA ** (-jnp.arange(half, dtype=F32) / half)
    ang = pos.astype(F32)[:, None] * inv[None, :]
    cos = jnp.cos(ang)[None, :, None, :]
    sin = jnp.sin(ang)[None, :, None, :]
    xf = x.astype(F32)
    x1, x2 = xf[..., :half], xf[..., half:]
    return jnp.concatenate([x1 * cos - x2 * sin, x1 * sin + x2 * cos], axis=-1).astype(x.dtype)


def sb_core(q, k, v, q_pos, k_pos, bias):
    logits = (jnp.einsum('bqhd,bkhd->bhqk', q, k).astype(F32) * (q.shape[-1] ** -0.5)
              + bias.astype(F32)[None, :, None, None])
    mask = k_pos[None, :] < q_pos[:, None]
    log_beta = jax.nn.log_sigmoid(logits)
    log_rest = jnp.where(mask, jax.nn.log_sigmoid(-logits), 0.0)
    after = lax.cumsum(log_rest, axis=3, reverse=True) - log_rest
    w = jnp.where(mask, jnp.exp(log_beta + after), 0.0)
    return jnp.einsum('bhqk,bkhd->bqhd', w, v.astype(F32)).astype(v.dtype)


def sb_prompt(q, k, v, bias):
    B, L, H, d = q.shape
    nb = L // SB_BLOCK
    qb = q.reshape(B, nb, SB_BLOCK, H, d).transpose(1, 0, 2, 3, 4)
    k_pos = jnp.arange(L)

    def block(args):
        qj, j = args
        q_pos = j * SB_BLOCK + jnp.arange(SB_BLOCK)
        return sb_core(qj, k, v, q_pos, k_pos, bias)

    out = lax.map(block, (qb, jnp.arange(nb)))
    return out.transpose(1, 0, 2, 3, 4).reshape(B, L, H, d)


def sb_sample(q, k, v, pool_k, pool_v, page_table, bias):
    _, L, H, d = q.shape
    past = page_table.shape[1] * PAGE_SIZE
    k_pos = jnp.arange(past + L)
    q_pos = past + jnp.arange(L)

    def one(args):
        qb, kb, vb, pages = args
        kp = pool_k[pages].reshape(past, H, d).astype(kb.dtype)
        vp = pool_v[pages].reshape(past, H, d).astype(vb.dtype)
        k_all = jnp.concatenate([kp, kb], axis=0)[None]
        v_all = jnp.concatenate([vp, vb], axis=0)[None]
        return sb_core(qb[None], k_all, v_all, q_pos, k_pos, bias)[0]

    return lax.map(one, (q, k, v, page_table))


def rglru_block(gate, xb, h0, conv0, conv_w, conv_b, wa, ba, wx, bx, lam):
    B, L, W = xb.shape
    xp = jnp.concatenate([conv0.astype(xb.dtype), xb], axis=1)
    conv_new = xp[:, L:]
    xpf = xp.astype(F32)
    cw = conv_w.astype(F32)
    y = conv_b.astype(F32) + sum(cw[j] * xpf[:, j:j + L] for j in range(CONV_W))
    yb = y.reshape(B, L, LRU_BLOCKS, LRU_BLOCK)
    r = jax.nn.sigmoid(jnp.einsum('blni,nij->blnj', yb, wa.astype(F32)).reshape(B, L, W) + ba.astype(F32))
    i = jax.nn.sigmoid(jnp.einsum('blni,nij->blnj', yb, wx.astype(F32)).reshape(B, L, W) + bx.astype(F32))
    log_a = -RG_C * r * jax.nn.softplus(-lam.astype(F32))
    a = jnp.exp(log_a)
    b = jnp.sqrt(-jnp.expm1(2.0 * log_a)) * (i * y)
    b = b.at[:, 0].add(a[:, 0] * h0.astype(F32))

    def combine(left, right):
        a1, b1 = left
        a2, b2 = right
        return a1 * a2, a2 * b1 + b2

    _, h = lax.associative_scan(combine, (a, b), axis=1)
    out = jax.nn.gelu(gate.astype(F32)) * h
    return out, h[:, -1], conv_new


def retention(q, k, v, s0):
    B, L, H, _ = q.shape
    dv = v.shape[-1]
    c = RET_CHUNK if L % RET_CHUNK == 0 else L
    n = L // c
    log_g = jnp.log1p(-jnp.exp2(-5.0 - jnp.arange(H, dtype=F32)))
    idx = jnp.arange(c, dtype=F32)
    diff = idx[:, None] - idx[None, :]
    causal = diff >= 0
    intra = jnp.where(causal[None], jnp.exp(jnp.where(causal, diff, 0.0)[None] * log_g[:, None, None]), 0.0)
    q_decay = jnp.exp((idx[:, None] + 1.0) * log_g[None, :])
    k_decay = jnp.exp((c - 1.0 - idx)[:, None] * log_g[None, :])
    s_decay = jnp.exp(c * log_g)

    def chunks(t):
        return t.astype(F32).reshape(B, n, c, H, t.shape[-1]).transpose(1, 0, 2, 3, 4)

    def step(s, inp):
        qc, kc, vc = inp
        att = jnp.einsum('bnhd,bmhd->bhnm', qc, kc) * intra
        o = (jnp.einsum('bhnm,bmhe->bnhe', att, vc)
             + jnp.einsum('bnhd,bhde->bnhe', qc, s) * q_decay[None, :, :, None])
        s = s_decay[:, None, None] * s + jnp.einsum('bmhd,bmhe->bhde', kc * k_decay[None, :, :, None], vc)
        return s, o

    s, o = lax.scan(step, s0.astype(F32), (chunks(q), chunks(k), chunks(v)))
    return o.transpose(1, 0, 2, 3, 4).reshape(B, L, H, dv), s


def sink_attention(q, k, v, mask, sinks):
    d = q.shape[-1]
    s = jnp.einsum('...qkgd,...skd->...kgqs', q, k).astype(F32) * (d ** -0.5)
    s = jnp.where(mask[..., None, None, :, :], s, -jnp.inf)
    sink = jnp.broadcast_to(sinks.astype(F32).reshape(SWA_KV_HEADS, SWA_GROUP, 1, 1), s.shape[:-1] + (1,))
    m = jnp.maximum(jnp.max(s, axis=-1, keepdims=True), sink)
    e = jnp.exp(s - m)
    p = e / (jnp.sum(e, axis=-1, keepdims=True) + jnp.exp(sink - m))
    return jnp.einsum('...kgqs,...skd->...qkgd', p, v.astype(F32))


def swa_prompt(q, k, v, sinks):
    B, L, _, d = q.shape
    nb = L // SWA_BLOCK
    qb = q.reshape(B, nb, SWA_BLOCK, SWA_KV_HEADS, SWA_GROUP, d)

    def band(t):
        tb = t.reshape(B, nb, SWA_BLOCK, SWA_KV_HEADS, d)
        prev = jnp.concatenate([jnp.zeros_like(tb[:, :1]), tb[:, :-1]], axis=1)
        return jnp.concatenate([prev, tb], axis=2)

    start = jnp.arange(nb)[:, None] * SWA_BLOCK
    q_pos = start + jnp.arange(SWA_BLOCK)[None]
    k_pos = start - SWA_BLOCK + jnp.arange(2 * SWA_BLOCK)[None]
    diff = q_pos[:, :, None] - k_pos[:, None, :]
    mask = (k_pos[:, None, :] >= 0) & (diff >= 0) & (diff < WINDOW)
    out = sink_attention(qb, band(k), band(v), mask[None], sinks)
    wb = min(WINDOW, L)
    return out.reshape(B, L, SWA_WIDTH), k[:, L - wb:], v[:, L - wb:]


def swa_sample(q, k, v, buf_k, buf_v, sinks):
    B, L, _, d = q.shape
    wb = buf_k.shape[1]
    k_all = jnp.concatenate([buf_k.astype(k.dtype), k], axis=1)
    v_all = jnp.concatenate([buf_v.astype(v.dtype), v], axis=1)
    k_pos = PAST_LEN - wb + jnp.arange(wb + L)
    q_pos = PAST_LEN + jnp.arange(L)
    diff = q_pos[:, None] - k_pos[None, :]
    mask = (k_pos[None, :] >= 0) & (diff >= 0) & (diff < WINDOW)
    out = sink_attention(q.reshape(B, L, SWA_KV_HEADS, SWA_GROUP, d), k_all, v_all, mask[None], sinks)
    return out.reshape(B, L, SWA_WIDTH), k_all[:, L:], v_all[:, L:]


def cross_attn(h, mk, mv, wq, wo):
    B, L, _ = h.shape
    q = (h @ wq).reshape(B, L, MEM_HEADS, HEAD_DIM)
    s = jnp.einsum('bqhd,bkhd->bhqk', q, mk).astype(F32) * (HEAD_DIM ** -0.5)
    p = jax.nn.softmax(s, axis=-1)
    o = jnp.einsum('bhqk,bkhd->bqhd', p, mv.astype(F32)).reshape(B, L, MEM_WIDTH).astype(h.dtype)
    return o @ wo


def mixer_ab(h, sb_attend, lru_h0, lru_c0, p):
    B, L, _ = h.shape
    q, k, v, gate, xb = split_cols(h @ p['ab_w_in'], [SB_WIDTH] * 3 + [LRU_WIDTH] * 2)
    q, k, v = [t.reshape(B, L, SB_HEADS, HEAD_DIM) for t in (q, k, v)]
    a_out = sb_attend(q, k, v).reshape(B, L, SB_WIDTH)
    b_out, h_last, conv_new = rglru_block(gate, xb, lru_h0, lru_c0, p['lru_conv_w'], p['lru_conv_b'],
                                          p['lru_wa'], p['lru_ba'], p['lru_wx'], p['lru_bx'], p['lru_lambda'])
    mix = jnp.concatenate([a_out.astype(h.dtype), b_out.astype(h.dtype)], axis=-1) @ p['ab_w_out']
    return mix, k, v, h_last, conv_new


def mixer_cd(h, pos, ret_s0, swa_attend, p):
    B, L, _ = h.shape
    qr, kr, vr, gr, qs, ks, vs = split_cols(h @ p['cd_w_in'],
                                            [RET_WIDTH] * 4 + [SWA_WIDTH, SWA_KV_WIDTH, SWA_KV_WIDTH])
    qr = rope(qr.reshape(B, L, RET_HEADS, RET_DK), pos)
    kr = rope(kr.reshape(B, L, RET_HEADS, RET_DK), pos) * (RET_DK ** -0.5)
    o, s_new = retention(qr, kr, vr.reshape(B, L, RET_HEADS, RET_DV), ret_s0)
    o = o * lax.rsqrt(jnp.mean(o * o, axis=-1, keepdims=True) + EPS) * p['ret_norm'].astype(F32)
    c_out = (o.reshape(B, L, RET_WIDTH) * jax.nn.silu(gr.astype(F32))).astype(h.dtype)
    qs = rope(qs.reshape(B, L, SWA_HEADS, HEAD_DIM), pos)
    ks = rope(ks.reshape(B, L, SWA_KV_HEADS, HEAD_DIM), pos)
    vs = vs.reshape(B, L, SWA_KV_HEADS, HEAD_DIM)
    d_out, k_buf, v_buf = swa_attend(qs, ks, vs)
    mix = jnp.concatenate([c_out, d_out.astype(h.dtype)], axis=-1) @ p['cd_w_out']
    return mix, s_new, k_buf, v_buf


def trunk(x, pos, mem_k, mem_v, sb_attend, swa_attend, lru_h0, lru_c0, ret_s0, p):
    for layer in range(DEPTH):
        x = x + 0.5 * swiglu(rmsnorm(x, p['ffn1_norm'][layer]), p['ffn1_wg'][layer],
                             p['ffn1_wu'][layer], p['ffn1_wd'][layer])
        h = rmsnorm(x, p['mix_norm'][layer])
        if layer % 2 == 0:
            mix, sb_k, sb_v, lru_h, lru_c = mixer_ab(h, sb_attend, lru_h0, lru_c0, p)
        else:
            mix, ret_s, swa_k, swa_v = mixer_cd(h, pos, ret_s0, swa_attend, p)
        x = x + mix
        x = x + cross_attn(rmsnorm(x, p['xattn_norm'][layer]), mem_k[layer], mem_v[layer],
                           p['xattn_wq'][layer], p['xattn_wo'][layer])
        x = x + 0.5 * swiglu(rmsnorm(x, p['ffn2_norm'][layer]), p['ffn2_wg'][layer],
                             p['ffn2_wu'][layer], p['ffn2_wd'][layer])
    return rmsnorm(x, p['final_norm']), sb_k, sb_v, lru_h, lru_c, ret_s, swa_k, swa_v


def setup_inputs(seed: int = 0) -> dict:
    key = jax.random.key(seed)
    keys = jax.random.split(key, 48)
    ks = [keys[i] for i in range(48)]

    def nrm(shape, scale=1.0):
        return jax.random.normal(ks.pop(), shape, F32) * scale

    def gain(shape):
        return 1.0 + nrm(shape, 0.02)

    n_pages = PAST_LEN // PAGE_SIZE
    n_used = DEC_BATCH * n_pages
    n_pool = n_used + max(1, n_used // 4)
    page_table = jax.random.permutation(ks.pop(), n_pool)[:n_used].reshape(DEC_BATCH, n_pages).astype(jnp.int32)
    wb = min(WINDOW, PAST_LEN)
    a_pow = jax.random.uniform(ks.pop(), (LRU_WIDTH,), F32, 0.9, 0.999)
    a_base = a_pow ** (1.0 / RG_C)
    lru_lambda = jnp.log(a_base) - jnp.log1p(-a_base)
    dsc = D_MODEL ** -0.5
    return {
        'x_prompt': nrm((BATCH, SEQ, D_MODEL)),
        'x_sample': nrm((DEC_BATCH, DEC_SEQ, D_MODEL)),
        'cache_sb_k': nrm((n_pool, PAGE_SIZE, SB_HEADS, HEAD_DIM)),
        'cache_sb_v': nrm((n_pool, PAGE_SIZE, SB_HEADS, HEAD_DIM)),
        'state_lru_h': nrm((DEC_BATCH, LRU_WIDTH), 0.5),
        'state_lru_conv': nrm((DEC_BATCH, CONV_W - 1, LRU_WIDTH)),
        'state_ret': nrm((DEC_BATCH, RET_HEADS, RET_DK, RET_DV), 0.1),
        'cache_swa_k': nrm((DEC_BATCH, wb, SWA_KV_HEADS, HEAD_DIM)),
        'cache_swa_v': nrm((DEC_BATCH, wb, SWA_KV_HEADS, HEAD_DIM)),
        'cache_mem_k': nrm((DEPTH, DEC_BATCH, N_MEM, MEM_HEADS, HEAD_DIM)),
        'cache_mem_v': nrm((DEPTH, DEC_BATCH, N_MEM, MEM_HEADS, HEAD_DIM)),
        'page_table': page_table,
        'mem_prompt': nrm((BATCH, N_MEM, D_MODEL)),
        'ffn1_norm': gain((DEPTH, D_MODEL)),
        'ffn1_wg': nrm((DEPTH, D_MODEL, D_FF), dsc),
        'ffn1_wu': nrm((DEPTH, D_MODEL, D_FF), dsc),
        'ffn1_wd': nrm((DEPTH, D_FF, D_MODEL), D_FF ** -0.5),
        'mix_norm': gain((DEPTH, D_MODEL)),
        'ab_w_in': nrm((D_MODEL, AB_IN), dsc),
        'ab_w_out': nrm((AB_OUT, D_MODEL), AB_OUT ** -0.5),
        'sb_bias': SB_BIAS_INIT + nrm((SB_HEADS,), 0.1),
        'lru_conv_w': nrm((CONV_W, LRU_WIDTH), CONV_W ** -0.5),
        'lru_conv_b': nrm((LRU_WIDTH,), 0.02),
        'lru_wa': nrm((LRU_BLOCKS, LRU_BLOCK, LRU_BLOCK), LRU_BLOCK ** -0.5),
        'lru_ba': nrm((LRU_WIDTH,), 0.02),
        'lru_wx': nrm((LRU_BLOCKS, LRU_BLOCK, LRU_BLOCK), LRU_BLOCK ** -0.5),
        'lru_bx': nrm((LRU_WIDTH,), 0.02),
        'lru_lambda': lru_lambda,
        'cd_w_in': nrm((D_MODEL, CD_IN), dsc),
        'cd_w_out': nrm((CD_OUT, D_MODEL), CD_OUT ** -0.5),
        'ret_norm': gain((RET_HEADS, RET_DV)),
        'swa_sinks': nrm((SWA_HEADS,), 0.5),
        'xattn_norm': gain((DEPTH, D_MODEL)),
        'mem_norm': gain((DEPTH, D_MODEL)),
        'xattn_wq': nrm((DEPTH, D_MODEL, MEM_WIDTH), dsc),
        'xattn_wk': nrm((DEPTH, D_MODEL, MEM_WIDTH), dsc),
        'xattn_wv': nrm((DEPTH, D_MODEL, MEM_WIDTH), dsc),
        'xattn_wo': nrm((DEPTH, MEM_WIDTH, D_MODEL), MEM_WIDTH ** -0.5),
        'ffn2_norm': gain((DEPTH, D_MODEL)),
        'ffn2_wg': nrm((DEPTH, D_MODEL, D_FF), dsc),
        'ffn2_wu': nrm((DEPTH, D_MODEL, D_FF), dsc),
        'ffn2_wd': nrm((DEPTH, D_FF, D_MODEL), D_FF ** -0.5),
        'final_norm': gain((D_MODEL,)),
    }


def reference(x_prompt, x_sample, cache_sb_k, cache_sb_v, state_lru_h, state_lru_conv, state_ret,
              cache_swa_k, cache_swa_v, cache_mem_k, cache_mem_v, page_table, mem_prompt,
              ffn1_norm, ffn1_wg, ffn1_wu, ffn1_wd, mix_norm,
              ab_w_in, ab_w_out, sb_bias, lru_conv_w, lru_conv_b, lru_wa, lru_ba, lru_wx, lru_bx, lru_lambda,
              cd_w_in, cd_w_out, ret_norm, swa_sinks,
              xattn_norm, mem_norm, xattn_wq, xattn_wk, xattn_wv, xattn_wo,
              ffn2_norm, ffn2_wg, ffn2_wu, ffn2_wd, final_norm):
    p = {
        'ffn1_norm': ffn1_norm, 'ffn1_wg': ffn1_wg, 'ffn1_wu': ffn1_wu, 'ffn1_wd': ffn1_wd,
        'mix_norm': mix_norm,
        'ab_w_in': ab_w_in, 'ab_w_out': ab_w_out, 'lru_conv_w': lru_conv_w, 'lru_conv_b': lru_conv_b,
        'lru_wa': lru_wa, 'lru_ba': lru_ba, 'lru_wx': lru_wx, 'lru_bx': lru_bx, 'lru_lambda': lru_lambda,
        'cd_w_in': cd_w_in, 'cd_w_out': cd_w_out, 'ret_norm': ret_norm,
        'xattn_norm': xattn_norm, 'xattn_wq': xattn_wq, 'xattn_wo': xattn_wo,
        'ffn2_norm': ffn2_norm, 'ffn2_wg': ffn2_wg, 'ffn2_wu': ffn2_wu, 'ffn2_wd': ffn2_wd,
        'final_norm': final_norm,
    }
    B = x_prompt.shape[0]
    n_mem = mem_prompt.shape[1]
    mem_n = rmsnorm(mem_prompt[None], mem_norm[:, None, None, :])
    mem_k_p = jnp.einsum('lbnd,lde->lbne', mem_n, xattn_wk).reshape(DEPTH, B, n_mem, MEM_HEADS, HEAD_DIM)
    mem_v_p = jnp.einsum('lbnd,lde->lbne', mem_n, xattn_wv).reshape(DEPTH, B, n_mem, MEM_HEADS, HEAD_DIM)

    (y_prompt, sb_k_p, sb_v_p, lru_h_p, lru_c_p, ret_p, swa_k_p, swa_v_p) = trunk(
        x_prompt, jnp.arange(x_prompt.shape[1]), mem_k_p, mem_v_p,
        functools.partial(sb_prompt, bias=sb_bias), functools.partial(swa_prompt, sinks=swa_sinks),
        jnp.zeros((B, LRU_WIDTH), F32), jnp.zeros((B, CONV_W - 1, LRU_WIDTH), x_prompt.dtype),
        jnp.zeros((B, RET_HEADS, RET_DK, RET_DV), F32), p)

    (y_sample, sb_k_s, sb_v_s, lru_h_s, lru_c_s, ret_s, swa_k_s, swa_v_s) = trunk(
        x_sample, PAST_LEN + jnp.arange(x_sample.shape[1]), cache_mem_k, cache_mem_v,
        functools.partial(sb_sample, pool_k=cache_sb_k, pool_v=cache_sb_v, page_table=page_table, bias=sb_bias),
        functools.partial(swa_sample, buf_k=cache_swa_k, buf_v=cache_swa_v, sinks=swa_sinks),
        state_lru_h, state_lru_conv, state_ret, p)

    return (y_prompt, y_sample, sb_k_p, sb_v_p, sb_k_s, sb_v_s, lru_h_p, lru_h_s, lru_c_p, lru_c_s,
            ret_p, ret_s, swa_k_p, swa_v_p, swa_k_s, swa_v_s, mem_k_p, mem_v_p)
```

```python
import functools

import jax
import jax.numpy as jnp
import numpy as np
from jax import lax
from jax.experimental import pallas as pl
from jax.experimental.pallas import tpu as pltpu

F32 = jnp.float32
BF16 = jnp.bfloat16

HEAD_DIM = 128
SB_HEADS = 8
SB_WIDTH = SB_HEADS * HEAD_DIM
LRU_WIDTH = 1024
LRU_BLOCKS = 8
LRU_BLOCK = LRU_WIDTH // LRU_BLOCKS
CONV_W = 4
RG_C = 8.0
RET_HEADS = 4
RET_DK = 256
RET_WIDTH = RET_HEADS * RET_DK
RET_CHUNK = 128
SWA_HEADS = 8
SWA_KV_HEADS = 2
SWA_GROUP = SWA_HEADS // SWA_KV_HEADS
SWA_WIDTH = SWA_HEADS * HEAD_DIM
SWA_KV_WIDTH = SWA_KV_HEADS * HEAD_DIM
WINDOW = 128
MEM_HEADS = 4
MEM_WIDTH = MEM_HEADS * HEAD_DIM
PAGE_SIZE = 128
ROPE_THETA = 10000.0
EPS = 1e-6

VMEM_LIMIT_V7X = 56 * 1024 * 1024


def _params(*semantics):
    return pltpu.CompilerParams(dimension_semantics=semantics, vmem_limit_bytes=VMEM_LIMIT_V7X)


def _dot(a, b):
    return jnp.dot(a, b, preferred_element_type=F32)


def _dot_nt(a, b):
    return lax.dot_general(a, b, (((1,), (1,)), ((), ())), preferred_element_type=F32)


def _rms(x, g):
    return x * lax.rsqrt(jnp.mean(x * x, axis=-1, keepdims=True) + EPS) * g


def _softplus(x):
    return jnp.maximum(x, 0.0) + jnp.log1p(jnp.exp(-jnp.abs(x)))


def _expm1(x):
    u = jnp.exp(x)
    one = u == 1.0
    return jnp.where(one, x, (u - 1.0) * x / jnp.where(one, 1.0, jnp.log(u)))


def _norm_matmul_kernel(x_ref, g_ref, w_ref, o_ref, h_ref):
    @pl.when(pl.program_id(1) == 0)
    def _():
        h_ref[...] = _rms(x_ref[...], g_ref[...]).astype(BF16)

    o_ref[...] = _dot(h_ref[...], w_ref[...]).astype(o_ref.dtype)


def norm_matmul(x, g, w, *, tm, tn, out_dtype=F32):
    M, D = x.shape
    N = w.shape[1]
    return pl.pallas_call(
        _norm_matmul_kernel,
        out_shape=jax.ShapeDtypeStruct((M, N), out_dtype),
        grid=(M // tm, N // tn),
        in_specs=[pl.BlockSpec((tm, D), lambda i, j: (i, 0)),
                  pl.BlockSpec((1, D), lambda i, j: (0, 0)),
                  pl.BlockSpec((D, tn), lambda i, j: (0, j))],
        out_specs=pl.BlockSpec((tm, tn), lambda i, j: (i, j)),
        scratch_shapes=[pltpu.VMEM((tm, D), BF16)],
        compiler_params=_params("parallel", "arbitrary"),
        name="norm_matmul",
    )(x, g.reshape(1, D), w)


def _ffn_kernel(x_ref, g_ref, wg_ref, wu_ref, wd_ref, o_ref, h_ref, acc_ref):
    f = pl.program_id(1)

    @pl.when(f == 0)
    def _():
        h_ref[...] = _rms(x_ref[...], g_ref[...]).astype(BF16)
        acc_ref[...] = jnp.zeros_like(acc_ref)

    h = h_ref[...]
    a = _dot(h, wg_ref[...])
    b = _dot(h, wu_ref[...])
    t = (a * jax.nn.sigmoid(a) * b).astype(BF16)
    acc_ref[...] += _dot(t, wd_ref[...])

    @pl.when(f == pl.num_programs(1) - 1)
    def _():
        o_ref[...] = x_ref[...] + 0.5 * acc_ref[...]


def ffn(x, g, wg, wu, wd, *, tm, tf):
    M, D = x.shape
    F = wg.shape[1]
    return pl.pallas_call(
        _ffn_kernel,
        out_shape=jax.ShapeDtypeStruct((M, D), F32),
        grid=(M // tm, F // tf),
        in_specs=[pl.BlockSpec((tm, D), lambda i, f: (i, 0)),
                  pl.BlockSpec((1, D), lambda i, f: (0, 0)),
                  pl.BlockSpec((D, tf), lambda i, f: (0, f)),
                  pl.BlockSpec((D, tf), lambda i, f: (0, f)),
                  pl.BlockSpec((tf, D), lambda i, f: (f, 0))],
        out_specs=pl.BlockSpec((tm, D), lambda i, f: (i, 0)),
        scratch_shapes=[pltpu.VMEM((tm, D), BF16), pltpu.VMEM((tm, D), F32)],
        compiler_params=_params("parallel", "arbitrary"),
        name="ffn",
    )(x, g.reshape(1, D), wg, wu, wd)


def _matmul_residual_kernel(x_ref, a_ref, w_ref, o_ref):
    o_ref[...] = x_ref[...] + _dot(a_ref[...].astype(BF16), w_ref[...])


def matmul_residual(x, a, w, *, tm, tn):
    M, N = x.shape
    K = a.shape[1]
    return pl.pallas_call(
        _matmul_residual_kernel,
        out_shape=jax.ShapeDtypeStruct((M, N), F32),
        grid=(M // tm, N // tn),
        in_specs=[pl.BlockSpec((tm, tn), lambda i, j: (i, j)),
                  pl.BlockSpec((tm, K), lambda i, j: (i, 0)),
                  pl.BlockSpec((K, tn), lambda i, j: (0, j))],
        out_specs=pl.BlockSpec((tm, tn), lambda i, j: (i, j)),
        compiler_params=_params("parallel", "parallel"),
        name="matmul_residual",
    )(x, a, w)


def _rmsnorm_kernel(x_ref, g_ref, o_ref):
    o_ref[...] = _rms(x_ref[...], g_ref[...])


def rmsnorm(x, g, *, tm):
    M, D = x.shape
    return pl.pallas_call(
        _rmsnorm_kernel,
        out_shape=jax.ShapeDtypeStruct((M, D), F32),
        grid=(M // tm,),
        in_specs=[pl.BlockSpec((tm, D), lambda i: (i, 0)), pl.BlockSpec((1, D), lambda i: (0, 0))],
        out_specs=pl.BlockSpec((tm, D), lambda i: (i, 0)),
        compiler_params=_params("parallel"),
        name="rmsnorm",
    )(x, g.reshape(1, D))


SB_SCALE = HEAD_DIM ** -0.5


def _split_dot(tri_first, x, tri):
    hi = x.astype(BF16)
    lo = (x - hi.astype(F32)).astype(BF16)
    if tri_first:
        return _dot(tri, hi) + _dot(tri, lo)
    return _dot(hi, tri) + _dot(lo, tri)


def _sb_prompt_tile(q, k, v, bias, tri, carry, mask):
    logits = _dot_nt(q, k) * SB_SCALE + bias
    log_beta = -_softplus(-logits)
    log_rest = log_beta - logits
    if mask is not None:
        log_rest = jnp.where(mask, log_rest, 0.0)
    after = carry + _split_dot(False, log_rest, tri)
    w = jnp.exp(log_beta + after)
    if mask is not None:
        w = jnp.where(mask, w, 0.0)
    return _dot(w.astype(BF16), v), carry + jnp.sum(log_rest, axis=-1, keepdims=True)


def _sb_prompt_kernel(bias_ref, q_ref, k_ref, v_ref, o_ref, kb_ref, vb_ref, *, blk, cast_rows):
    h = pl.program_id(0)
    i = pl.program_id(1)
    L = k_ref.shape[0]

    @pl.when(i == 0)
    def _():
        def cast(n, c):
            s = pl.multiple_of(n * cast_rows, cast_rows)
            kb_ref[pl.ds(s, cast_rows), :] = k_ref[pl.ds(s, cast_rows), :].astype(BF16)
            vb_ref[pl.ds(s, cast_rows), :] = v_ref[pl.ds(s, cast_rows), :].astype(BF16)
            return c
        lax.fori_loop(0, L // cast_rows, cast, 0)

    q = q_ref[...].astype(BF16)
    bias = bias_ref[h]
    row = lax.broadcasted_iota(jnp.int32, (blk, blk), 0)
    col = lax.broadcasted_iota(jnp.int32, (blk, blk), 1)
    tri = jnp.where(row > col, 1.0, 0.0).astype(BF16)
    start = pl.multiple_of(i * blk, blk)
    acc, carry = _sb_prompt_tile(q, kb_ref[pl.ds(start, blk), :], vb_ref[pl.ds(start, blk), :], bias, tri,
                                 jnp.zeros((blk, 1), F32), col < row)

    def body(n, c):
        acc, carry = c
        s = pl.multiple_of((i - 1 - n) * blk, blk)
        a, carry = _sb_prompt_tile(q, kb_ref[pl.ds(s, blk), :], vb_ref[pl.ds(s, blk), :], bias, tri, carry, None)
        return acc + a, carry

    acc, carry = lax.fori_loop(0, i, body, (acc, carry))
    o_ref[...] = acc.astype(o_ref.dtype)


def sb_prompt(qkv, bias, *, L, blk=128):
    kern = functools.partial(_sb_prompt_kernel, blk=blk, cast_rows=min(L, 512))
    return pl.pallas_call(
        kern,
        out_shape=jax.ShapeDtypeStruct((L, SB_WIDTH), BF16),
        grid=(SB_HEADS, L // blk),
        in_specs=[pl.BlockSpec(memory_space=pltpu.SMEM),
                  pl.BlockSpec((blk, HEAD_DIM), lambda h, i: (i, h)),
                  pl.BlockSpec((L, HEAD_DIM), lambda h, i: (0, SB_HEADS + h)),
                  pl.BlockSpec((L, HEAD_DIM), lambda h, i: (0, 2 * SB_HEADS + h))],
        out_specs=pl.BlockSpec((blk, HEAD_DIM), lambda h, i: (i, h)),
        scratch_shapes=[pltpu.VMEM((L, HEAD_DIM), BF16), pltpu.VMEM((L, HEAD_DIM), BF16)],
        compiler_params=_params("arbitrary", "arbitrary"),
        name="sb_prompt",
    )(bias, qkv, qkv, qkv)


SB_COLS = 128


def _sb_sample_tile(k, v, qbd, bias_row, tri_t, carry, mask):
    logits = _dot(k.astype(BF16), qbd) * SB_SCALE + bias_row
    log_beta = -_softplus(-logits)
    log_rest = log_beta - logits
    if mask is not None:
        log_rest = jnp.where(mask, log_rest, 0.0)
    after = carry + _split_dot(True, log_rest, tri_t)
    w = jnp.exp(log_beta + after)
    if mask is not None:
        w = jnp.where(mask, w, 0.0)
    n_rows = SB_HEADS * 4
    wt = jnp.transpose(w)[:n_rows].astype(BF16)
    return _dot(wt, v.astype(BF16)), carry + jnp.sum(log_rest, axis=0, keepdims=True)


def _sb_sample_kernel(pt_ref, qbd_ref, bias_ref, knew_ref, vnew_ref, kp_ref, vp_ref, o_ref, acc_ref, carry_ref,
                      *, n_q):
    p = pl.program_id(1)
    keys = kp_ref.shape[1]
    row = lax.broadcasted_iota(jnp.int32, (keys, SB_COLS), 0)
    col = lax.broadcasted_iota(jnp.int32, (keys, SB_COLS), 1)
    tri_t = jnp.where(col > row, 1.0, 0.0).astype(BF16)
    qbd = qbd_ref[0]
    bias_row = bias_ref[...]

    @pl.when(p == 0)
    def _():
        a, c = _sb_sample_tile(knew_ref[0], vnew_ref[0], qbd, bias_row, tri_t, jnp.zeros((1, SB_COLS), F32),
                               row < col % n_q)
        acc_ref[...] = a
        carry_ref[...] = c

    @pl.when(p > 0)
    def _():
        a, c = _sb_sample_tile(kp_ref[0], vp_ref[0], qbd, bias_row, tri_t, carry_ref[...], None)
        acc_ref[...] += a
        carry_ref[...] = c

    @pl.when(p == pl.num_programs(1) - 1)
    def _():
        for h in range(SB_HEADS):
            o_ref[0, :, h * HEAD_DIM:(h + 1) * HEAD_DIM] = (
                acc_ref[h * n_q:(h + 1) * n_q, h * HEAD_DIM:(h + 1) * HEAD_DIM].astype(o_ref.dtype))


def sb_sample(q, k, v, pool_k, pool_v, page_table, bias):
    B, n_q, _ = q.shape
    n_pages = page_table.shape[1]
    assert n_q * SB_HEADS <= SB_COLS and n_q <= PAGE_SIZE
    eye = jnp.eye(SB_HEADS, dtype=F32)
    qh = q.reshape(B, n_q, SB_HEADS, HEAD_DIM)
    qbd = jnp.einsum('bthd,hg->bhdgt', qh, eye).reshape(B, SB_WIDTH, SB_HEADS * n_q)
    qbd = jnp.pad(qbd, ((0, 0), (0, 0), (0, SB_COLS - SB_HEADS * n_q))).astype(BF16)
    bias_row = jnp.pad(jnp.repeat(bias.astype(F32), n_q), (0, SB_COLS - SB_HEADS * n_q)).reshape(1, SB_COLS)
    knew = jnp.pad(k, ((0, 0), (0, PAGE_SIZE - n_q), (0, 0)))
    vnew = jnp.pad(v, ((0, 0), (0, PAGE_SIZE - n_q), (0, 0)))

    def page_map(b, p, pt):
        return (pt[b * n_pages + n_pages - 1 - jnp.maximum(p - 1, 0)], 0, 0)

    per_seq = lambda b, p, pt: (b, 0, 0)
    grid_spec = pltpu.PrefetchScalarGridSpec(
        num_scalar_prefetch=1,
        grid=(B, n_pages + 1),
        in_specs=[pl.BlockSpec((1, SB_WIDTH, SB_COLS), per_seq),
                  pl.BlockSpec((1, SB_COLS), lambda b, p, pt: (0, 0)),
                  pl.BlockSpec((1, PAGE_SIZE, SB_WIDTH), per_seq),
                  pl.BlockSpec((1, PAGE_SIZE, SB_WIDTH), per_seq),
                  pl.BlockSpec((1, PAGE_SIZE, SB_WIDTH), page_map),
                  pl.BlockSpec((1, PAGE_SIZE, SB_WIDTH), page_map)],
        out_specs=pl.BlockSpec((1, n_q, SB_WIDTH), per_seq),
        scratch_shapes=[pltpu.VMEM((SB_HEADS * n_q, SB_WIDTH), F32), pltpu.VMEM((1, SB_COLS), F32)],
    )
    return pl.pallas_call(
        functools.partial(_sb_sample_kernel, n_q=n_q),
        out_shape=jax.ShapeDtypeStruct((B, n_q, SB_WIDTH), BF16),
        grid_spec=grid_spec,
        compiler_params=_params("parallel", "arbitrary"),
        name="sb_sample",
    )(page_table.reshape(-1), qbd, bias_row, knew, vnew, pool_k, pool_v)


def _lru_coeffs(y, wa_ref, wx_ref, ba, bx, sp_lam):
    yb = y.astype(BF16)
    r_parts, i_parts = [], []
    for n in range(LRU_BLOCKS):
        yn = yb[:, n * LRU_BLOCK:(n + 1) * LRU_BLOCK]
        r_parts.append(_dot(yn, wa_ref[n]))
        i_parts.append(_dot(yn, wx_ref[n]))
    r = jax.nn.sigmoid(jnp.concatenate(r_parts, axis=-1) + ba)
    i = jax.nn.sigmoid(jnp.concatenate(i_parts, axis=-1) + bx)
    log_a = -RG_C * r * sp_lam
    return jnp.exp(log_a), jnp.sqrt(-_expm1(2.0 * log_a)) * (i * y)


LRU_PAD = 8


def _lru_prompt_kernel(gate_ref, xb_ref, conv0_ref, h0_ref, cw_ref, cb_ref, wa_ref, wx_ref, ba_ref, bx_ref,
                       lam_ref, o_ref, hlast_ref, cnew_ref, ext_ref, h_ref, a_ref, b_ref, hs_ref, *, T):
    i = pl.program_id(0)
    tail = CONV_W - 1

    @pl.when(i == 0)
    def _():
        ext_ref[LRU_PAD - tail:LRU_PAD, :] = conv0_ref[...]
        h_ref[...] = h0_ref[...]

    ext_ref[LRU_PAD:LRU_PAD + T, :] = xb_ref[...]
    y = cb_ref[...]
    for j in range(CONV_W):
        y = y + cw_ref[j:j + 1, :] * ext_ref[LRU_PAD - tail + j:LRU_PAD - tail + j + T, :]
    a, b = _lru_coeffs(y, wa_ref, wx_ref, ba_ref[...], bx_ref[...], _softplus(-lam_ref[...]))
    a_ref[...] = a
    b_ref[...] = b

    def step(t, h):
        h = a_ref[pl.ds(t, 1), :] * h + b_ref[pl.ds(t, 1), :]
        hs_ref[pl.ds(t, 1), :] = h
        return h

    h = lax.fori_loop(0, T, step, h_ref[...], unroll=8)
    h_ref[...] = h
    o_ref[...] = (jax.nn.gelu(gate_ref[...]) * hs_ref[...]).astype(o_ref.dtype)
    last_rows = ext_ref[LRU_PAD + T - tail:LRU_PAD + T, :]
    ext_ref[LRU_PAD - tail:LRU_PAD, :] = last_rows

    @pl.when(i == pl.num_programs(0) - 1)
    def _():
        hlast_ref[...] = h
        cnew_ref[...] = last_rows


def _lru_weights(p):
    row = lambda t: t.reshape(1, LRU_WIDTH)
    return (p['lru_conv_w'], row(p['lru_conv_b']), p['lru_wa'].astype(BF16), p['lru_wx'].astype(BF16),
            row(p['lru_ba']), row(p['lru_bx']), row(p['lru_lambda']))


def _lru_weight_specs():
    zero2 = lambda i: (0, 0)
    zero3 = lambda i: (0, 0, 0)
    vec = pl.BlockSpec((1, LRU_WIDTH), zero2)
    mat = pl.BlockSpec((LRU_BLOCKS, LRU_BLOCK, LRU_BLOCK), zero3)
    return [pl.BlockSpec((CONV_W, LRU_WIDTH), zero2), vec, mat, mat, vec, vec, vec]


def lru_prompt(proj, gate_col, xb_col, conv0, h0, p, *, L, T=256):
    T = min(T, L)
    assert L % T == 0 and T >= CONV_W - 1
    W = LRU_WIDTH
    fixed = lambda i: (0, 0)
    return pl.pallas_call(
        functools.partial(_lru_prompt_kernel, T=T),
        out_shape=(jax.ShapeDtypeStruct((L, W), BF16), jax.ShapeDtypeStruct((1, W), F32),
                   jax.ShapeDtypeStruct((CONV_W - 1, W), F32)),
        grid=(L // T,),
        in_specs=[pl.BlockSpec((T, W), lambda i: (i, gate_col)), pl.BlockSpec((T, W), lambda i: (i, xb_col)),
                  pl.BlockSpec((CONV_W - 1, W), fixed), pl.BlockSpec((1, W), fixed)] + _lru_weight_specs(),
        out_specs=(pl.BlockSpec((T, W), lambda i: (i, 0)), pl.BlockSpec((1, W), fixed),
                   pl.BlockSpec((CONV_W - 1, W), fixed)),
        scratch_shapes=[pltpu.VMEM((LRU_PAD + T, W), F32), pltpu.VMEM((1, W), F32), pltpu.VMEM((T, W), F32),
                        pltpu.VMEM((T, W), F32), pltpu.VMEM((T, W), F32)],
        compiler_params=_params("arbitrary"),
        name="lru_prompt",
    )(proj, proj, conv0, h0, *_lru_weights(p))


def _lru_sample_kernel(gate_ref, xb_ref, conv0_ref, h0_ref, cw_ref, cb_ref, wa_ref, wx_ref, ba_ref, bx_ref,
                       lam_ref, o_ref, hlast_ref, cnew_ref, *, L):
    tail = CONV_W - 1
    xs = [conv0_ref[j] for j in range(tail)] + [xb_ref[t] for t in range(L)]
    sp_lam = _softplus(-lam_ref[...])
    h = h0_ref[...]
    for t in range(L):
        y = cb_ref[...]
        for j in range(CONV_W):
            y = y + cw_ref[j:j + 1, :] * xs[t + j]
        a, b = _lru_coeffs(y, wa_ref, wx_ref, ba_ref[...], bx_ref[...], sp_lam)
        h = a * h + b
        o_ref[t] = (jax.nn.gelu(gate_ref[t]) * h).astype(o_ref.dtype)
    hlast_ref[...] = h
    for j in range(tail):
        cnew_ref[j] = xs[L + j]


def lru_sample(gate_t, xb_t, conv0_t, h0, p, *, tb=64):
    L, B, W = xb_t.shape
    tb = min(tb, B)
    assert B % tb == 0
    tmaj = lambda n: pl.BlockSpec((n, tb, W), lambda i: (0, i, 0))
    rows = pl.BlockSpec((tb, W), lambda i: (i, 0))
    return pl.pallas_call(
        functools.partial(_lru_sample_kernel, L=L),
        out_shape=(jax.ShapeDtypeStruct((L, B, W), BF16), jax.ShapeDtypeStruct((B, W), F32),
                   jax.ShapeDtypeStruct((CONV_W - 1, B, W), F32)),
        grid=(B // tb,),
        in_specs=[tmaj(L), tmaj(L), tmaj(CONV_W - 1), rows] + _lru_weight_specs(),
        out_specs=(tmaj(L), rows, tmaj(CONV_W - 1)),
        compiler_params=_params("parallel"),
        name="lru_sample",
    )(gate_t, xb_t, conv0_t, h0, *_lru_weights(p))


def rope_tables(pos, half):
    inv = ROPE_THETA ** (-jnp.arange(half, dtype=F32) / half)
    ang = pos.astype(F32)[:, None] * inv[None, :]
    return jnp.cos(ang), jnp.sin(ang)


def _dot_tn(a, b):
    return lax.dot_general(a, b, (((0,), (0,)), ((), ())), preferred_element_type=F32)


def _rope_halves(x, cos, sin):
    half = x.shape[-1] // 2
    x1, x2 = x[:, :half], x[:, half:]
    return jnp.concatenate([x1 * cos - x2 * sin, x1 * sin + x2 * cos], axis=-1)


def _retention_kernel(sdec_ref, q_ref, k_ref, v_ref, g_ref, cos_ref, sin_ref, intra_ref, qd_ref, kd_ref, norm_ref,
                      s0_ref, o_ref, sout_ref, s_ref):
    n = pl.program_id(1)

    @pl.when(n == 0)
    def _():
        s_ref[...] = s0_ref[0]

    cos, sin = cos_ref[...], sin_ref[...]
    for h in range(RET_HEADS):
        cols = slice(h * RET_DK, (h + 1) * RET_DK)
        q = _rope_halves(q_ref[0, :, cols], cos, sin)
        k = _rope_halves(k_ref[0, :, cols], cos, sin) * (RET_DK ** -0.5)
        v = v_ref[0, :, cols].astype(BF16)
        qb = q.astype(BF16)
        s = s_ref[h]
        att = _dot_nt(qb, k.astype(BF16)) * intra_ref[h]
        o = _dot(att.astype(BF16), v) + _dot(qb, s.astype(BF16)) * qd_ref[h]
        s_ref[h] = sdec_ref[h] * s + _dot_tn((k * kd_ref[h]).astype(BF16), v)
        o = o * lax.rsqrt(jnp.mean(o * o, axis=-1, keepdims=True) + EPS) * norm_ref[h:h + 1, :]
        g = g_ref[0, :, cols]
        o_ref[0, :, cols] = (o * (g * jax.nn.sigmoid(g))).astype(o_ref.dtype)

    @pl.when(n == pl.num_programs(1) - 1)
    def _():
        sout_ref[0] = s_ref[...]


def retention(proj, cols, cos, sin, s0, ret_norm, *, L):
    B = proj.shape[0]
    c = RET_CHUNK if L % RET_CHUNK == 0 else L
    H = RET_HEADS
    log_g = jnp.log1p(-jnp.exp2(-5.0 - jnp.arange(H, dtype=F32)))
    idx = jnp.arange(c, dtype=F32)
    diff = idx[:, None] - idx[None, :]
    causal = diff >= 0
    intra = jnp.where(causal[None], jnp.exp(jnp.where(causal, diff, 0.0)[None] * log_g[:, None, None]), 0.0)
    q_decay = jnp.exp((idx[None, :] + 1.0) * log_g[:, None])[:, :, None]
    k_decay = jnp.exp((c - 1.0 - idx)[None, :] * log_g[:, None])[:, :, None]
    s_decay = jnp.exp(c * log_g)
    blk = lambda col: pl.BlockSpec((1, c, RET_WIDTH), lambda b, n, col=col: (b, n, col))
    table = pl.BlockSpec((c, RET_DK // 2), lambda b, n: (n, 0))
    fixed3 = lambda shape: pl.BlockSpec(shape, lambda b, n: (0, 0, 0))
    state = pl.BlockSpec((1, H, RET_DK, RET_DK), lambda b, n: (b, 0, 0, 0))
    return pl.pallas_call(
        _retention_kernel,
        out_shape=(jax.ShapeDtypeStruct((B, L, RET_WIDTH), BF16), jax.ShapeDtypeStruct(s0.shape, F32)),
        grid=(B, L // c),
        in_specs=[pl.BlockSpec(memory_space=pltpu.SMEM)] + [blk(col) for col in cols]
                 + [table, table, fixed3((H, c, c)), fixed3((H, c, 1)), fixed3((H, c, 1)),
                    pl.BlockSpec((H, RET_DK), lambda b, n: (0, 0)), state],
        out_specs=(pl.BlockSpec((1, c, RET_WIDTH), lambda b, n: (b, n, 0)), state),
        scratch_shapes=[pltpu.VMEM((H, RET_DK, RET_DK), F32)],
        compiler_params=_params("parallel", "arbitrary"),
        name="retention",
    )(s_decay, proj, proj, proj, proj, cos, sin, intra, q_decay, k_decay, ret_norm, s0)


ATTN_SCALE = HEAD_DIM ** -0.5


def rope_tables_full(pos):
    cos, sin = rope_tables(pos, HEAD_DIM // 2)
    return jnp.concatenate([cos, cos], axis=-1), jnp.concatenate([-sin, sin], axis=-1)


def _rope_roll(x, cos_full, sin_signed):
    return x * cos_full + pltpu.roll(x, HEAD_DIM // 2, 1) * sin_signed


def _sink_softmax(s, mask, sink):
    s = jnp.where(mask, s, -jnp.inf)
    m = jnp.maximum(jnp.max(s, axis=-1, keepdims=True), sink)
    e = jnp.exp(s - m)
    return e / (jnp.sum(e, axis=-1, keepdims=True) + jnp.exp(sink - m))


def _swa_prompt_kernel(sinks_ref, q_ref, kc_ref, kp_ref, vc_ref, vp_ref, cc_ref, sc_ref, cp_ref, sp_ref,
                       o_ref, krot_ref, *, blk):
    j = pl.program_id(0)
    r = lax.broadcasted_iota(jnp.int32, (blk, 2 * blk), 0)
    c = lax.broadcasted_iota(jnp.int32, (blk, 2 * blk), 1)
    diff = blk + r - c
    mask = ((j - 1) * blk + c >= 0) & (diff >= 0) & (diff < WINDOW)
    cos_c, sin_c, cos_p, sin_p = cc_ref[...], sc_ref[...], cp_ref[...], sp_ref[...]
    for g in range(SWA_KV_HEADS):
        kv = slice(g * HEAD_DIM, (g + 1) * HEAD_DIM)
        k_cur = _rope_roll(kc_ref[:, kv], cos_c, sin_c)
        krot_ref[:, kv] = k_cur
        k_all = jnp.concatenate([_rope_roll(kp_ref[:, kv], cos_p, sin_p), k_cur], axis=0).astype(BF16)
        v_all = jnp.concatenate([vp_ref[:, kv], vc_ref[:, kv]], axis=0).astype(BF16)
        for hh in range(SWA_GROUP):
            h = g * SWA_GROUP + hh
            cols = slice(h * HEAD_DIM, (h + 1) * HEAD_DIM)
            q = _rope_roll(q_ref[:, cols], cos_c, sin_c).astype(BF16)
            p = _sink_softmax(_dot_nt(q, k_all) * ATTN_SCALE, mask, sinks_ref[h])
            o_ref[:, cols] = _dot(p.astype(BF16), v_all).astype(o_ref.dtype)


def swa_prompt(proj, q_col, k_col, v_col, cos_full, sin_signed, sinks, *, L, blk=128):
    assert L % blk == 0 and blk == WINDOW
    cur = lambda j: (j, 0)
    prev = lambda j: (jnp.maximum(j - 1, 0), 0)
    kv = lambda col, prv: pl.BlockSpec((blk, SWA_KV_WIDTH),
                                       (lambda j: (jnp.maximum(j - 1, 0), col)) if prv else (lambda j: (j, col)))
    table = lambda m: pl.BlockSpec((blk, HEAD_DIM), m)
    return pl.pallas_call(
        functools.partial(_swa_prompt_kernel, blk=blk),
        out_shape=(jax.ShapeDtypeStruct((L, SWA_WIDTH), BF16), jax.ShapeDtypeStruct((L, SWA_KV_WIDTH), F32)),
        grid=(L // blk,),
        in_specs=[pl.BlockSpec(memory_space=pltpu.SMEM),
                  pl.BlockSpec((blk, SWA_WIDTH), lambda j: (j, q_col)),
                  kv(k_col, False), kv(k_col, True), kv(v_col, False), kv(v_col, True),
                  table(cur), table(cur), table(prev), table(prev)],
        out_specs=(pl.BlockSpec((blk, SWA_WIDTH), cur), pl.BlockSpec((blk, SWA_KV_WIDTH), cur)),
        compiler_params=_params("parallel"),
        name="swa_prompt",
    )(sinks, proj, proj, proj, proj, proj, cos_full, sin_signed, cos_full, sin_signed)


def _swa_sample_kernel(q_ref, knew_ref, vnew_ref, bk_ref, bv_ref, cq_ref, sq_ref, ck_ref, sk_ref, sink_ref,
                       o_ref, knr_ref, kall_ref, vall_ref, *, n_new, past_len):
    wb = bk_ref.shape[1]
    rows = q_ref.shape[2]
    r = lax.broadcasted_iota(jnp.int32, (rows, 2 * wb), 0)
    c = lax.broadcasted_iota(jnp.int32, (rows, 2 * wb), 1)
    diff = r // SWA_GROUP + wb - c
    mask = (past_len - wb + c >= 0) & (diff >= 0) & (diff < WINDOW) & (c < wb + n_new)
    for g in range(SWA_KV_HEADS):
        kv = slice(g * HEAD_DIM, (g + 1) * HEAD_DIM)
        k_new = _rope_roll(knew_ref[0, :, kv], ck_ref[...], sk_ref[...])
        knr_ref[0, :, kv] = k_new
        kall_ref[0:wb, :] = bk_ref[0, :, kv]
        kall_ref[wb:2 * wb, :] = jnp.zeros((wb, HEAD_DIM), F32)
        kall_ref[wb:wb + n_new, :] = k_new
        vall_ref[0:wb, :] = bv_ref[0, :, kv]
        vall_ref[wb:2 * wb, :] = jnp.zeros((wb, HEAD_DIM), F32)
        vall_ref[wb:wb + n_new, :] = vnew_ref[0, :, kv]
        q = _rope_roll(q_ref[0, g], cq_ref[...], sq_ref[...]).astype(BF16)
        p = _sink_softmax(_dot_nt(q, kall_ref[...].astype(BF16)) * ATTN_SCALE, mask, sink_ref[g])
        o_ref[0, g] = _dot(p.astype(BF16), vall_ref[...].astype(BF16)).astype(o_ref.dtype)


def swa_sample(qs, ks, vs, buf_k, buf_v, sinks, *, past_len):
    B, n, _ = qs.shape
    wb = buf_k.shape[1]
    assert n <= wb
    rows = n * SWA_GROUP
    pos = past_len + jnp.arange(n)
    cos_k, sin_k = rope_tables_full(pos)
    cos_q, sin_q = jnp.repeat(cos_k, SWA_GROUP, axis=0), jnp.repeat(sin_k, SWA_GROUP, axis=0)
    q_g = qs.reshape(B, n, SWA_KV_HEADS, SWA_GROUP, HEAD_DIM).transpose(0, 2, 1, 3, 4).reshape(
        B, SWA_KV_HEADS, rows, HEAD_DIM)
    sink_g = jnp.tile(sinks.astype(F32).reshape(SWA_KV_HEADS, 1, SWA_GROUP), (1, n, 1)).reshape(
        SWA_KV_HEADS, rows, 1)
    per_b3 = lambda shape: pl.BlockSpec((1,) + shape, lambda b: (b, 0, 0))
    per_b4 = pl.BlockSpec((1, SWA_KV_HEADS, rows, HEAD_DIM), lambda b: (b, 0, 0, 0))
    fixed2 = lambda shape: pl.BlockSpec(shape, lambda b: (0, 0))
    o, k_rot = pl.pallas_call(
        functools.partial(_swa_sample_kernel, n_new=n, past_len=past_len),
        out_shape=(jax.ShapeDtypeStruct((B, SWA_KV_HEADS, rows, HEAD_DIM), BF16),
                   jax.ShapeDtypeStruct((B, n, SWA_KV_WIDTH), F32)),
        grid=(B,),
        in_specs=[per_b4, per_b3((n, SWA_KV_WIDTH)), per_b3((n, SWA_KV_WIDTH)),
                  per_b3((wb, SWA_KV_WIDTH)), per_b3((wb, SWA_KV_WIDTH)),
                  fixed2((rows, HEAD_DIM)), fixed2((rows, HEAD_DIM)), fixed2((n, HEAD_DIM)), fixed2((n, HEAD_DIM)),
                  pl.BlockSpec((SWA_KV_HEADS, rows, 1), lambda b: (0, 0, 0))],
        out_specs=(per_b4, per_b3((n, SWA_KV_WIDTH))),
        scratch_shapes=[pltpu.VMEM((2 * wb, HEAD_DIM), F32), pltpu.VMEM((2 * wb, HEAD_DIM), F32)],
        compiler_params=_params("parallel"),
        name="swa_sample",
    )(q_g, ks, vs, buf_k, buf_v, cos_q, sin_q, cos_k, sin_k, sink_g)
    d_out = o.reshape(B, SWA_KV_HEADS, n, SWA_GROUP, HEAD_DIM).transpose(0, 2, 1, 3, 4).reshape(B, n, SWA_WIDTH)
    return d_out, k_rot


def _xattn_kernel(q_ref, k_ref, v_ref, o_ref):
    for h in range(MEM_HEADS):
        cols = slice(h * HEAD_DIM, (h + 1) * HEAD_DIM)
        s = _dot_nt(q_ref[0, :, cols].astype(BF16), k_ref[0, :, cols].astype(BF16)) * ATTN_SCALE
        e = jnp.exp(s - jnp.max(s, axis=-1, keepdims=True))
        p = e / jnp.sum(e, axis=-1, keepdims=True)
        o_ref[0, :, cols] = _dot(p.astype(BF16), v_ref[0, :, cols].astype(BF16)).astype(o_ref.dtype)


def cross_attention(q, mem_k, mem_v, *, L, tq):
    B = q.shape[0]
    n_mem = mem_k.shape[1]
    mem = pl.BlockSpec((1, n_mem, MEM_WIDTH), lambda b, i: (b, 0, 0))
    rows = pl.BlockSpec((1, tq, MEM_WIDTH), lambda b, i: (b, i, 0))
    return pl.pallas_call(
        _xattn_kernel,
        out_shape=jax.ShapeDtypeStruct((B, L, MEM_WIDTH), BF16),
        grid=(B, L // tq),
        in_specs=[rows, mem, mem],
        out_specs=rows,
        compiler_params=_params("parallel", "parallel"),
        name="cross_attention",
    )(q, mem_k, mem_v)


ROW_TILE = 512
COL_TILE = 512
XATTN_PAD = 8


def kernel(x_prompt, x_sample, cache_sb_k, cache_sb_v, state_lru_h, state_lru_conv, state_ret, cache_swa_k, cache_swa_v, cache_mem_k, cache_mem_v, page_table, mem_prompt, ffn1_norm, ffn1_wg, ffn1_wu, ffn1_wd, mix_norm, ab_w_in, ab_w_out, sb_bias, lru_conv_w, lru_conv_b, lru_wa, lru_ba, lru_wx, lru_bx, lru_lambda, cd_w_in, cd_w_out, ret_norm, swa_sinks, xattn_norm, mem_norm, xattn_wq, xattn_wk, xattn_wv, xattn_wo, ffn2_norm, ffn2_wg, ffn2_wu, ffn2_wd, final_norm):
    Bp, Lp, D = x_prompt.shape
    Bs, Ls, _ = x_sample.shape
    assert Bp == 1
    Ms = Bs * Ls
    depth = ffn1_norm.shape[0]
    n_mem = mem_prompt.shape[1]
    n_pool = cache_sb_k.shape[0]
    past_len = page_table.shape[1] * PAGE_SIZE
    wb = cache_swa_k.shape[1]
    bf = lambda w: w.astype(BF16)
    lru_p = dict(lru_conv_w=lru_conv_w, lru_conv_b=lru_conv_b, lru_wa=lru_wa, lru_wx=lru_wx, lru_ba=lru_ba,
                 lru_bx=lru_bx, lru_lambda=lru_lambda)
    dense_in = functools.partial(norm_matmul, tm=ROW_TILE, tn=COL_TILE)
    dense_ffn = functools.partial(ffn, tm=ROW_TILE, tf=COL_TILE)
    dense_out = functools.partial(matmul_residual, tm=ROW_TILE, tn=2 * COL_TILE)
    tmajor = lambda t: jnp.transpose(t, (1, 0, 2))

    mem_kv = [norm_matmul(mem_prompt[0], mem_norm[l], bf(jnp.concatenate([xattn_wk[l], xattn_wv[l]], axis=1)),
                          tm=n_mem, tn=COL_TILE) for l in range(depth)]
    mem_k_p = jnp.stack([kv[:, :MEM_WIDTH] for kv in mem_kv])[:, None]
    mem_v_p = jnp.stack([kv[:, MEM_WIDTH:] for kv in mem_kv])[:, None]

    x = jnp.concatenate([x_prompt.reshape(Lp, D), x_sample.reshape(Ms, D)], axis=0)
    for layer in range(depth):
        x = dense_ffn(x, ffn1_norm[layer], bf(ffn1_wg[layer]), bf(ffn1_wu[layer]), bf(ffn1_wd[layer]))
        if layer % 2 == 0:
            proj = dense_in(x, mix_norm[layer], bf(ab_w_in))
            sproj = proj[Lp:].reshape(Bs, Ls, -1)
            q_s, k_s, v_s = [sproj[..., i * SB_WIDTH:(i + 1) * SB_WIDTH] for i in range(3)]
            a_p = sb_prompt(proj, sb_bias, L=Lp)
            a_s = sb_sample(q_s, k_s, v_s, cache_sb_k.reshape(n_pool, PAGE_SIZE, SB_WIDTH),
                            cache_sb_v.reshape(n_pool, PAGE_SIZE, SB_WIDTH), page_table, sb_bias)
            b_p, lru_h_p, lru_c_p = lru_prompt(proj, 3, 4, jnp.zeros((CONV_W - 1, LRU_WIDTH), F32),
                                               jnp.zeros((1, LRU_WIDTH), F32), lru_p, L=Lp)
            b_s, lru_h_s, lru_c_s = lru_sample(tmajor(sproj[..., 3 * SB_WIDTH:3 * SB_WIDTH + LRU_WIDTH]),
                                               tmajor(sproj[..., 3 * SB_WIDTH + LRU_WIDTH:]),
                                               tmajor(state_lru_conv), state_lru_h, lru_p)
            mix_in = jnp.concatenate([jnp.concatenate([a_p, b_p], axis=1),
                                      jnp.concatenate([a_s.reshape(Ms, SB_WIDTH),
                                                       tmajor(b_s).reshape(Ms, LRU_WIDTH)], axis=1)], axis=0)
            x = dense_out(x, mix_in, bf(ab_w_out))
            sb_k_p = proj[:Lp, SB_WIDTH:2 * SB_WIDTH].reshape(1, Lp, SB_HEADS, HEAD_DIM)
            sb_v_p = proj[:Lp, 2 * SB_WIDTH:3 * SB_WIDTH].reshape(1, Lp, SB_HEADS, HEAD_DIM)
            sb_k_s = k_s.reshape(Bs, Ls, SB_HEADS, HEAD_DIM)
            sb_v_s = v_s.reshape(Bs, Ls, SB_HEADS, HEAD_DIM)
            lru_c_p = lru_c_p[None]
            lru_c_s = tmajor(lru_c_s)
        else:
            proj = dense_in(x, mix_norm[layer], bf(cd_w_in))
            sproj = proj[Lp:].reshape(Bs, Ls, -1)
            pos_p, pos_s = jnp.arange(Lp), past_len + jnp.arange(Ls)
            c_p, ret_p = retention(proj[None], (0, 1, 2, 3), *rope_tables(pos_p, RET_DK // 2),
                                   jnp.zeros((1, RET_HEADS, RET_DK, RET_DK), F32), ret_norm, L=Lp)
            c_s, ret_s = retention(sproj, (0, 1, 2, 3), *rope_tables(pos_s, RET_DK // 2), state_ret, ret_norm, L=Ls)
            kv0 = 4 * RET_WIDTH + SWA_WIDTH
            d_p, k_rot_p = swa_prompt(proj, 4 * RET_WIDTH // SWA_WIDTH, kv0 // SWA_KV_WIDTH, kv0 // SWA_KV_WIDTH + 1,
                                      *rope_tables_full(pos_p), swa_sinks, L=Lp)
            d_s, k_rot_s = swa_sample(sproj[..., 4 * RET_WIDTH:kv0], sproj[..., kv0:kv0 + SWA_KV_WIDTH],
                                      sproj[..., kv0 + SWA_KV_WIDTH:], cache_swa_k.reshape(Bs, wb, SWA_KV_WIDTH),
                                      cache_swa_v.reshape(Bs, wb, SWA_KV_WIDTH), swa_sinks, past_len=past_len)
            mix_in = jnp.concatenate([jnp.concatenate([c_p[0], d_p], axis=1),
                                      jnp.concatenate([c_s.reshape(Ms, RET_WIDTH),
                                                       d_s.reshape(Ms, SWA_WIDTH)], axis=1)], axis=0)
            x = dense_out(x, mix_in, bf(cd_w_out))
            wp = min(WINDOW, Lp)
            swa_k_p = k_rot_p[Lp - wp:].reshape(1, wp, SWA_KV_HEADS, HEAD_DIM)
            swa_v_p = proj[Lp - wp:Lp, kv0 + SWA_KV_WIDTH:].reshape(1, wp, SWA_KV_HEADS, HEAD_DIM)
            kv_heads = lambda t: t.reshape(Bs, Ls, SWA_KV_HEADS, HEAD_DIM)
            swa_k_s = jnp.concatenate([cache_swa_k, kv_heads(k_rot_s)], axis=1)[:, Ls:]
            swa_v_s = jnp.concatenate([cache_swa_v, kv_heads(sproj[..., kv0 + SWA_KV_WIDTH:])], axis=1)[:, Ls:]
        qx = dense_in(x, xattn_norm[layer], bf(xattn_wq[layer]))
        o_p = cross_attention(qx[None], mem_k_p[layer], mem_v_p[layer], L=Lp, tq=ROW_TILE)
        q_s = jnp.pad(qx[Lp:].reshape(Bs, Ls, MEM_WIDTH), ((0, 0), (0, XATTN_PAD - Ls), (0, 0)))
        o_s = cross_attention(q_s, cache_mem_k[layer].reshape(Bs, n_mem, MEM_WIDTH),
                              cache_mem_v[layer].reshape(Bs, n_mem, MEM_WIDTH), L=XATTN_PAD, tq=XATTN_PAD)
        x = dense_out(x, jnp.concatenate([o_p[0], o_s[:, :Ls].reshape(Ms, MEM_WIDTH)], axis=0), bf(xattn_wo[layer]))
        x = dense_ffn(x, ffn2_norm[layer], bf(ffn2_wg[layer]), bf(ffn2_wu[layer]), bf(ffn2_wd[layer]))
    y = rmsnorm(x, final_norm, tm=ROW_TILE)
    mem_heads = lambda t: t.reshape(depth, 1, n_mem, MEM_HEADS, HEAD_DIM)
    return (y[:Lp].reshape(1, Lp, D), y[Lp:].reshape(Bs, Ls, D), sb_k_p, sb_v_p, sb_k_s, sb_v_s,
            lru_h_p, lru_h_s, lru_c_p, lru_c_s, ret_p, ret_s, swa_k_p, swa_v_p, swa_k_s, swa_v_s,
            mem_heads(mem_k_p), mem_heads(mem_v_p))
```

```python
import functools

import jax
import jax.numpy as jnp
import numpy as np
from jax import lax
from jax.experimental import pallas as pl
from jax.experimental.pallas import tpu as pltpu

F32 = jnp.float32
BF16 = jnp.bfloat16

HEAD_DIM = 128
SB_HEADS = 8
SB_WIDTH = SB_HEADS * HEAD_DIM
LRU_WIDTH = 1024
LRU_BLOCKS = 8
LRU_BLOCK = LRU_WIDTH // LRU_BLOCKS
CONV_W = 4
RG_C = 8.0
RET_HEADS = 4
RET_DK = 256
RET_WIDTH = RET_HEADS * RET_DK
RET_CHUNK = 128
SWA_HEADS = 8
SWA_KV_HEADS = 2
SWA_GROUP = SWA_HEADS // SWA_KV_HEADS
SWA_WIDTH = SWA_HEADS * HEAD_DIM
SWA_KV_WIDTH = SWA_KV_HEADS * HEAD_DIM
WINDOW = 128
MEM_HEADS = 4
MEM_WIDTH = MEM_HEADS * HEAD_DIM
PAGE_SIZE = 128
ROPE_THETA = 10000.0
EPS = 1e-6

VMEM_LIMIT_V7X = 56 * 1024 * 1024


def _params(*semantics):
    return pltpu.CompilerParams(dimension_semantics=semantics, vmem_limit_bytes=VMEM_LIMIT_V7X)


def _dot(a, b):
    return jnp.dot(a, b, preferred_element_type=F32)


def _dot_nt(a, b):
    return lax.dot_general(a, b, (((1,), (1,)), ((), ())), preferred_element_type=F32)


def _rms(x, g):
    return x * lax.rsqrt(jnp.mean(x * x, axis=-1, keepdims=True) + EPS) * g


def _softplus(x):
    return jnp.maximum(x, 0.0) + jnp.log1p(jnp.exp(-jnp.abs(x)))


def _expm1(x):
    u = jnp.exp(x)
    one = u == 1.0
    return jnp.where(one, x, (u - 1.0) * x / jnp.where(one, 1.0, jnp.log(u)))


def _norm_matmul_kernel(x_ref, g_ref, w_ref, o_ref, h_ref):
    @pl.when(pl.program_id(1) == 0)
    def _():
        h_ref[...] = _rms(x_ref[...], g_ref[...]).astype(BF16)

    o_ref[...] = _dot(h_ref[...], w_ref[...]).astype(o_ref.dtype)


def norm_matmul(x, g, w, *, tm, tn, out_dtype=F32):
    M, D = x.shape
    N = w.shape[1]
    return pl.pallas_call(
        _norm_matmul_kernel,
        out_shape=jax.ShapeDtypeStruct((M, N), out_dtype),
        grid=(M // tm, N // tn),
        in_specs=[pl.BlockSpec((tm, D), lambda i, j: (i, 0)),
                  pl.BlockSpec((1, D), lambda i, j: (0, 0)),
                  pl.BlockSpec((D, tn), lambda i, j: (0, j))],
        out_specs=pl.BlockSpec((tm, tn), lambda i, j: (i, j)),
        scratch_shapes=[pltpu.VMEM((tm, D), BF16)],
        compiler_params=_params("parallel", "arbitrary"),
        name="norm_matmul",
    )(x, g.reshape(1, D), w)


def _ffn_kernel(x_ref, g_ref, wg_ref, wu_ref, wd_ref, o_ref, h_ref, acc_ref):
    f = pl.program_id(1)

    @pl.when(f == 0)
    def _():
        h_ref[...] = _rms(x_ref[...], g_ref[...]).astype(BF16)
        acc_ref[...] = jnp.zeros_like(acc_ref)

    h = h_ref[...]
    a = _dot(h, wg_ref[...])
    b = _dot(h, wu_ref[...])
    t = (a * jax.nn.sigmoid(a) * b).astype(BF16)
    acc_ref[...] += _dot(t, wd_ref[...])

    @pl.when(f == pl.num_programs(1) - 1)
    def _():
        o_ref[...] = x_ref[...] + 0.5 * acc_ref[...]


def ffn(x, g, wg, wu, wd, *, tm, tf):
    M, D = x.shape
    F = wg.shape[1]
    return pl.pallas_call(
        _ffn_kernel,
        out_shape=jax.ShapeDtypeStruct((M, D), F32),
        grid=(M // tm, F // tf),
        in_specs=[pl.BlockSpec((tm, D), lambda i, f: (i, 0)),
                  pl.BlockSpec((1, D), lambda i, f: (0, 0)),
                  pl.BlockSpec((D, tf), lambda i, f: (0, f)),
                  pl.BlockSpec((D, tf), lambda i, f: (0, f)),
                  pl.BlockSpec((tf, D), lambda i, f: (f, 0))],
        out_specs=pl.BlockSpec((tm, D), lambda i, f: (i, 0)),
        scratch_shapes=[pltpu.VMEM((tm, D), BF16), pltpu.VMEM((tm, D), F32)],
        compiler_params=_params("parallel", "arbitrary"),
        name="ffn",
    )(x, g.reshape(1, D), wg, wu, wd)


def _matmul_residual_kernel(x_ref, a_ref, w_ref, o_ref):
    o_ref[...] = x_ref[...] + _dot(a_ref[...].astype(BF16), w_ref[...])


def matmul_residual(x, a, w, *, tm, tn):
    M, N = x.shape
    K = a.shape[1]
    return pl.pallas_call(
        _matmul_residual_kernel,
        out_shape=jax.ShapeDtypeStruct((M, N), F32),
        grid=(M // tm, N // tn),
        in_specs=[pl.BlockSpec((tm, tn), lambda i, j: (i, j)),
                  pl.BlockSpec((tm, K), lambda i, j: (i, 0)),
                  pl.BlockSpec((K, tn), lambda i, j: (0, j))],
        out_specs=pl.BlockSpec((tm, tn), lambda i, j: (i, j)),
        compiler_params=_params("parallel", "parallel"),
        name="matmul_residual",
    )(x, a, w)


def _rmsnorm_kernel(x_ref, g_ref, o_ref):
    o_ref[...] = _rms(x_ref[...], g_ref[...])


def rmsnorm(x, g, *, tm):
    M, D = x.shape
    return pl.pallas_call(
        _rmsnorm_kernel,
        out_shape=jax.ShapeDtypeStruct((M, D), F32),
        grid=(M // tm,),
        in_specs=[pl.BlockSpec((tm, D), lambda i: (i, 0)), pl.BlockSpec((1, D), lambda i: (0, 0))],
        out_specs=pl.BlockSpec((tm, D), lambda i: (i, 0)),
        compiler_params=_params("parallel"),
        name="rmsnorm",
    )(x, g.reshape(1, D))


SB_SCALE = HEAD_DIM ** -0.5


def _split_dot(tri_first, x, tri):
    hi = x.astype(BF16)
    lo = (x - hi.astype(F32)).astype(BF16)
    if tri_first:
        return _dot(tri, hi) + _dot(tri, lo)
    return _dot(hi, tri) + _dot(lo, tri)


def _sb_scores(logits, mask):
    sp = jnp.log(1.0 + jnp.exp(-jnp.abs(logits)))
    log_beta = jnp.minimum(logits, 0.0) - sp
    log_rest = log_beta - logits
    if mask is not None:
        log_rest = jnp.where(mask, log_rest, 0.0)
    return log_beta, log_rest


def _sb_prompt_tiles(q, k, v, bias, tri, carry, mask, n_sub):
    blk = tri.shape[0]
    log_beta, log_rest = _sb_scores(_dot_nt(q, k) * SB_SCALE + bias, mask)
    ws = [None] * n_sub
    for s in reversed(range(n_sub)):
        cols = slice(s * blk, (s + 1) * blk)
        after = carry + _split_dot(False, log_rest[:, cols], tri)
        w = jnp.exp(log_beta[:, cols] + after)
        ws[s] = (w if mask is None else jnp.where(mask, w, 0.0)).astype(BF16)
        carry = carry + jnp.sum(log_rest[:, cols], axis=-1, keepdims=True)
    w_all = ws[0] if n_sub == 1 else jnp.concatenate(ws, axis=-1)
    return _dot(w_all, v), carry


SB_SWEEP = 4


def _sb_prompt_kernel(bias_ref, q_ref, k_ref, v_ref, o_ref, kb_ref, vb_ref, *, blk):
    h = pl.program_id(0)
    i = pl.program_id(1)
    L = k_ref.shape[0]

    @pl.when(i == 0)
    def _():
        def cast(n, c):
            s = pl.multiple_of(n * blk, blk)
            kb_ref[pl.ds(s, blk), :] = k_ref[pl.ds(s, blk), :].astype(BF16)
            vb_ref[pl.ds(s, blk), :] = v_ref[pl.ds(s, blk), :].astype(BF16)
            return c
        lax.fori_loop(0, L // blk, cast, 0)

    q = q_ref[...].astype(BF16)
    bias = bias_ref[h]
    row = lax.broadcasted_iota(jnp.int32, (blk, blk), 0)
    col = lax.broadcasted_iota(jnp.int32, (blk, blk), 1)
    tri = jnp.where(row > col, 1.0, 0.0).astype(BF16)

    def sweep(first_block, n_sub, mask, state):
        acc, carry = state
        s = pl.multiple_of(first_block * blk, blk)
        a, carry = _sb_prompt_tiles(q, kb_ref[pl.ds(s, n_sub * blk), :], vb_ref[pl.ds(s, n_sub * blk), :], bias, tri,
                                    carry, mask, n_sub)
        return acc + a, carry

    state = sweep(i, 1, col < row, (jnp.zeros((blk, HEAD_DIM), F32), jnp.zeros((blk, 1), F32)))
    odd = i % SB_SWEEP
    state = lax.fori_loop(0, odd, lambda n, st: sweep(i - 1 - n, 1, None, st), state)
    state = lax.fori_loop(0, i // SB_SWEEP, lambda n, st: sweep(i - odd - SB_SWEEP * (n + 1), SB_SWEEP, None, st),
                          state)
    o_ref[...] = state[0].astype(o_ref.dtype)


def sb_prompt(qkv, bias, *, L, blk=256):
    blk = min(blk, L)
    assert L % blk == 0
    kern = functools.partial(_sb_prompt_kernel, blk=blk)
    return pl.pallas_call(
        kern,
        out_shape=jax.ShapeDtypeStruct((L, SB_WIDTH), BF16),
        grid=(SB_HEADS, L // blk),
        in_specs=[pl.BlockSpec(memory_space=pltpu.SMEM),
                  pl.BlockSpec((blk, HEAD_DIM), lambda h, i: (i, h)),
                  pl.BlockSpec((L, HEAD_DIM), lambda h, i: (0, SB_HEADS + h)),
                  pl.BlockSpec((L, HEAD_DIM), lambda h, i: (0, 2 * SB_HEADS + h))],
        out_specs=pl.BlockSpec((blk, HEAD_DIM), lambda h, i: (i, h)),
        scratch_shapes=[pltpu.VMEM((L, HEAD_DIM), BF16), pltpu.VMEM((L, HEAD_DIM), BF16)],
        compiler_params=_params("arbitrary", "arbitrary"),
        name="sb_prompt",
    )(bias, qkv, qkv, qkv)


SB_LANES = 128
SB_PACK = 4
SB_STEP = 8


def _page_rows(ref):
    keys = ref.shape[1] // SB_HEADS
    return jnp.concatenate([ref[0, pl.ds(h, keys, stride=SB_HEADS), :] for h in range(SB_HEADS)], axis=-1)


def _fold_lanes(t, width):
    out = t
    for j in range(1, SB_LANES // width):
        out = out + pltpu.roll(t, j * width, 1)
    return out


def _sb_sample_tile(ks, vs, qbd_ref, bias_row, tri_t, carry, mask, width):
    logits = None
    for i, k in enumerate(ks):
        part = _dot(k.astype(BF16), qbd_ref[i])
        logits = part if logits is None else logits + part
    log_beta, log_rest = _sb_scores(logits * SB_SCALE + bias_row, mask)
    tot = jnp.sum(log_rest, axis=0, keepdims=True)
    lane = lax.broadcasted_iota(jnp.int32, tot.shape, 1)
    within = jnp.zeros_like(tot)
    for j in range(1, len(ks)):
        within = within + jnp.where(lane >= j * width, pltpu.roll(tot, j * width, 1), 0.0)
    after = carry + within + _split_dot(True, log_rest, tri_t)
    w = jnp.exp(log_beta + after)
    if mask is not None:
        w = jnp.where(mask, w, 0.0)
    wt = jnp.transpose(w).astype(BF16)
    out = None
    for i, v in enumerate(vs):
        part = _dot(wt[i * width:(i + 1) * width], v.astype(BF16))
        out = part if out is None else out + part
    return out, carry + _fold_lanes(tot, width)


def _sb_sample_kernel(pt_ref, qbd_ref, bias_ref, knew_ref, vnew_ref, *refs, n_q):
    kp, vp = refs[:SB_STEP], refs[SB_STEP:2 * SB_STEP]
    o_ref, acc_ref, carry_ref, qbd4_ref = refs[2 * SB_STEP:]
    s = pl.program_id(1)
    keys = knew_ref.shape[1]
    width = SB_HEADS * n_q
    row = lax.broadcasted_iota(jnp.int32, (keys, SB_LANES), 0)
    col = lax.broadcasted_iota(jnp.int32, (keys, SB_LANES), 1)
    tri_t = jnp.where(col > row, 1.0, 0.0).astype(BF16)
    bias_row = bias_ref[...]

    @pl.when(s == 0)
    def _():
        q = qbd_ref[0]
        for i in range(SB_PACK):
            qbd4_ref[i] = (q if i == 0 else pltpu.roll(q, i * width, 1)).astype(BF16)
        mask = (row < col % n_q) & (col < width)
        a, c = _sb_sample_tile([knew_ref[0]], [vnew_ref[0]], qbd4_ref, bias_row, tri_t,
                               jnp.zeros((1, SB_LANES), F32), mask, width)
        acc_ref[...] = a
        carry_ref[...] = c

    carry = carry_ref[...]
    acc = acc_ref[...]
    for g in range(SB_STEP // SB_PACK):
        sel = range(g * SB_PACK, (g + 1) * SB_PACK)
        a, carry = _sb_sample_tile([_page_rows(kp[j]) for j in sel], [_page_rows(vp[j]) for j in sel], qbd4_ref,
                                   bias_row, tri_t, carry, None, width)
        acc = acc + a
    acc_ref[...] = acc
    carry_ref[...] = carry

    @pl.when(s == pl.num_programs(1) - 1)
    def _():
        for h in range(SB_HEADS):
            o_ref[0, :, h * HEAD_DIM:(h + 1) * HEAD_DIM] = (
                acc[h * n_q:(h + 1) * n_q, h * HEAD_DIM:(h + 1) * HEAD_DIM].astype(o_ref.dtype))


def sb_sample(q, k, v, pool_k, pool_v, page_table, bias):
    B, n_q, _ = q.shape
    n_pool = pool_k.shape[0]
    n_pages = page_table.shape[1]
    width = SB_HEADS * n_q
    assert width * SB_PACK == SB_LANES and n_q <= PAGE_SIZE and n_pages % SB_STEP == 0 and SB_STEP % SB_PACK == 0
    eye = jnp.eye(SB_HEADS, dtype=F32)
    qh = q.reshape(B, n_q, SB_HEADS, HEAD_DIM)
    qbd = jnp.einsum('bthd,hg->bhdgt', qh, eye).reshape(B, SB_WIDTH, width)
    qbd = jnp.pad(qbd, ((0, 0), (0, 0), (0, SB_LANES - width)))
    bias_row = jnp.tile(jnp.repeat(bias.astype(F32), n_q), SB_PACK).reshape(1, SB_LANES)
    knew = jnp.pad(k, ((0, 0), (0, PAGE_SIZE - n_q), (0, 0)))
    vnew = jnp.pad(v, ((0, 0), (0, PAGE_SIZE - n_q), (0, 0)))
    pool_k = pool_k.reshape(n_pool, PAGE_SIZE * SB_HEADS, HEAD_DIM)
    pool_v = pool_v.reshape(n_pool, PAGE_SIZE * SB_HEADS, HEAD_DIM)

    def page_spec(j):
        return pl.BlockSpec((1, PAGE_SIZE * SB_HEADS, HEAD_DIM),
                            lambda b, s, pt: (pt[b * n_pages + n_pages - 1 - s * SB_STEP - j], 0, 0))

    per_seq = lambda b, s, pt: (b, 0, 0)
    grid_spec = pltpu.PrefetchScalarGridSpec(
        num_scalar_prefetch=1,
        grid=(B, n_pages // SB_STEP),
        in_specs=[pl.BlockSpec((1, SB_WIDTH, SB_LANES), per_seq),
                  pl.BlockSpec((1, SB_LANES), lambda b, s, pt: (0, 0)),
                  pl.BlockSpec((1, PAGE_SIZE, SB_WIDTH), per_seq),
                  pl.BlockSpec((1, PAGE_SIZE, SB_WIDTH), per_seq)]
                 + [page_spec(j) for j in range(SB_STEP)] * 2,
        out_specs=pl.BlockSpec((1, n_q, SB_WIDTH), per_seq),
        scratch_shapes=[pltpu.VMEM((width, SB_WIDTH), F32), pltpu.VMEM((1, SB_LANES), F32),
                        pltpu.VMEM((SB_PACK, SB_WIDTH, SB_LANES), BF16)],
    )
    return pl.pallas_call(
        functools.partial(_sb_sample_kernel, n_q=n_q),
        out_shape=jax.ShapeDtypeStruct((B, n_q, SB_WIDTH), BF16),
        grid_spec=grid_spec,
        compiler_params=_params("parallel", "arbitrary"),
        name="sb_sample",
    )(page_table.reshape(-1), qbd, bias_row, knew, vnew, *([pool_k] * SB_STEP), *([pool_v] * SB_STEP))


def _lru_coeffs(y, wa_ref, wx_ref, ba, bx, sp_lam):
    yb = y.astype(BF16)
    r_parts, i_parts = [], []
    for n in range(LRU_BLOCKS):
        yn = yb[:, n * LRU_BLOCK:(n + 1) * LRU_BLOCK]
        r_parts.append(_dot(yn, wa_ref[n]))
        i_parts.append(_dot(yn, wx_ref[n]))
    r = jax.nn.sigmoid(jnp.concatenate(r_parts, axis=-1) + ba)
    i = jax.nn.sigmoid(jnp.concatenate(i_parts, axis=-1) + bx)
    log_a = -RG_C * r * sp_lam
    return jnp.exp(log_a), jnp.sqrt(-_expm1(2.0 * log_a)) * (i * y)


LRU_PAD = 8


def _lru_prompt_kernel(gate_ref, xb_ref, conv0_ref, h0_ref, cw_ref, cb_ref, wa_ref, wx_ref, ba_ref, bx_ref,
                       lam_ref, o_ref, hlast_ref, cnew_ref, ext_ref, h_ref, a_ref, b_ref, hs_ref, *, T):
    i = pl.program_id(0)
    tail = CONV_W - 1

    @pl.when(i == 0)
    def _():
        ext_ref[LRU_PAD - tail:LRU_PAD, :] = conv0_ref[...]
        h_ref[...] = h0_ref[...]

    ext_ref[LRU_PAD:LRU_PAD + T, :] = xb_ref[...]
    y = cb_ref[...]
    for j in range(CONV_W):
        y = y + cw_ref[j:j + 1, :] * ext_ref[LRU_PAD - tail + j:LRU_PAD - tail + j + T, :]
    a, b = _lru_coeffs(y, wa_ref, wx_ref, ba_ref[...], bx_ref[...], _softplus(-lam_ref[...]))
    a_ref[...] = a
    b_ref[...] = b

    def step(t, h):
        h = a_ref[pl.ds(t, 1), :] * h + b_ref[pl.ds(t, 1), :]
        hs_ref[pl.ds(t, 1), :] = h
        return h

    h = lax.fori_loop(0, T, step, h_ref[...], unroll=8)
    h_ref[...] = h
    o_ref[...] = (jax.nn.gelu(gate_ref[...]) * hs_ref[...]).astype(o_ref.dtype)
    last_rows = ext_ref[LRU_PAD + T - tail:LRU_PAD + T, :]
    ext_ref[LRU_PAD - tail:LRU_PAD, :] = last_rows

    @pl.when(i == pl.num_programs(0) - 1)
    def _():
        hlast_ref[...] = h
        cnew_ref[...] = last_rows


def _lru_weights(p):
    row = lambda t: t.reshape(1, LRU_WIDTH)
    return (p['lru_conv_w'], row(p['lru_conv_b']), p['lru_wa'].astype(BF16), p['lru_wx'].astype(BF16),
            row(p['lru_ba']), row(p['lru_bx']), row(p['lru_lambda']))


def _lru_weight_specs():
    zero2 = lambda i: (0, 0)
    zero3 = lambda i: (0, 0, 0)
    vec = pl.BlockSpec((1, LRU_WIDTH), zero2)
    mat = pl.BlockSpec((LRU_BLOCKS, LRU_BLOCK, LRU_BLOCK), zero3)
    return [pl.BlockSpec((CONV_W, LRU_WIDTH), zero2), vec, mat, mat, vec, vec, vec]


def lru_prompt(proj, gate_col, xb_col, conv0, h0, p, *, L, T=256):
    T = min(T, L)
    assert L % T == 0 and T >= CONV_W - 1
    W = LRU_WIDTH
    fixed = lambda i: (0, 0)
    return pl.pallas_call(
        functools.partial(_lru_prompt_kernel, T=T),
        out_shape=(jax.ShapeDtypeStruct((L, W), BF16), jax.ShapeDtypeStruct((1, W), F32),
                   jax.ShapeDtypeStruct((CONV_W - 1, W), F32)),
        grid=(L // T,),
        in_specs=[pl.BlockSpec((T, W), lambda i: (i, gate_col)), pl.BlockSpec((T, W), lambda i: (i, xb_col)),
                  pl.BlockSpec((CONV_W - 1, W), fixed), pl.BlockSpec((1, W), fixed)] + _lru_weight_specs(),
        out_specs=(pl.BlockSpec((T, W), lambda i: (i, 0)), pl.BlockSpec((1, W), fixed),
                   pl.BlockSpec((CONV_W - 1, W), fixed)),
        scratch_shapes=[pltpu.VMEM((LRU_PAD + T, W), F32), pltpu.VMEM((1, W), F32), pltpu.VMEM((T, W), F32),
                        pltpu.VMEM((T, W), F32), pltpu.VMEM((T, W), F32)],
        compiler_params=_params("arbitrary"),
        name="lru_prompt",
    )(proj, proj, conv0, h0, *_lru_weights(p))


def _lru_sample_kernel(gate_ref, xb_ref, conv0_ref, h0_ref, cw_ref, cb_ref, wa_ref, wx_ref, ba_ref, bx_ref,
                       lam_ref, o_ref, hlast_ref, cnew_ref, *, L):
    tail = CONV_W - 1
    xs = [conv0_ref[j] for j in range(tail)] + [xb_ref[t] for t in range(L)]
    sp_lam = _softplus(-lam_ref[...])
    h = h0_ref[...]
    for t in range(L):
        y = cb_ref[...]
        for j in range(CONV_W):
            y = y + cw_ref[j:j + 1, :] * xs[t + j]
        a, b = _lru_coeffs(y, wa_ref, wx_ref, ba_ref[...], bx_ref[...], sp_lam)
        h = a * h + b
        o_ref[t] = (jax.nn.gelu(gate_ref[t]) * h).astype(o_ref.dtype)
    hlast_ref[...] = h
    for j in range(tail):
        cnew_ref[j] = xs[L + j]


def lru_sample(gate_t, xb_t, conv0_t, h0, p, *, tb=64):
    L, B, W = xb_t.shape
    tb = min(tb, B)
    assert B % tb == 0
    tmaj = lambda n: pl.BlockSpec((n, tb, W), lambda i: (0, i, 0))
    rows = pl.BlockSpec((tb, W), lambda i: (i, 0))
    return pl.pallas_call(
        functools.partial(_lru_sample_kernel, L=L),
        out_shape=(jax.ShapeDtypeStruct((L, B, W), BF16), jax.ShapeDtypeStruct((B, W), F32),
                   jax.ShapeDtypeStruct((CONV_W - 1, B, W), F32)),
        grid=(B // tb,),
        in_specs=[tmaj(L), tmaj(L), tmaj(CONV_W - 1), rows] + _lru_weight_specs(),
        out_specs=(tmaj(L), rows, tmaj(CONV_W - 1)),
        compiler_params=_params("parallel"),
        name="lru_sample",
    )(gate_t, xb_t, conv0_t, h0, *_lru_weights(p))


def rope_tables(pos, half):
    inv = ROPE_THETA ** (-jnp.arange(half, dtype=F32) / half)
    ang = pos.astype(F32)[:, None] * inv[None, :]
    return jnp.cos(ang), jnp.sin(ang)


def _dot_tn(a, b):
    return lax.dot_general(a, b, (((0,), (0,)), ((), ())), preferred_element_type=F32)


def _rope_halves(x, cos, sin):
    half = x.shape[-1] // 2
    x1, x2 = x[:, :half], x[:, half:]
    return jnp.concatenate([x1 * cos - x2 * sin, x1 * sin + x2 * cos], axis=-1)


def _retention_kernel(sdec_ref, q_ref, k_ref, v_ref, g_ref, cos_ref, sin_ref, intra_ref, qd_ref, kd_ref, norm_ref,
                      s0_ref, o_ref, sout_ref, s_ref):
    n = pl.program_id(1)

    @pl.when(n == 0)
    def _():
        s_ref[...] = s0_ref[0]

    cos, sin = cos_ref[...], sin_ref[...]
    for h in range(RET_HEADS):
        cols = slice(h * RET_DK, (h + 1) * RET_DK)
        q = _rope_halves(q_ref[0, :, cols], cos, sin)
        k = _rope_halves(k_ref[0, :, cols], cos, sin) * (RET_DK ** -0.5)
        v = v_ref[0, :, cols].astype(BF16)
        qb = q.astype(BF16)
        s = s_ref[h]
        att = _dot_nt(qb, k.astype(BF16)) * intra_ref[h]
        o = _dot(att.astype(BF16), v) + _dot(qb, s.astype(BF16)) * qd_ref[h]
        s_ref[h] = sdec_ref[h] * s + _dot_tn((k * kd_ref[h]).astype(BF16), v)
        o = o * lax.rsqrt(jnp.mean(o * o, axis=-1, keepdims=True) + EPS) * norm_ref[h:h + 1, :]
        g = g_ref[0, :, cols]
        o_ref[0, :, cols] = (o * (g * jax.nn.sigmoid(g))).astype(o_ref.dtype)

    @pl.when(n == pl.num_programs(1) - 1)
    def _():
        sout_ref[0] = s_ref[...]


def retention(proj, cols, cos, sin, s0, ret_norm, *, L):
    B = proj.shape[0]
    c = RET_CHUNK if L % RET_CHUNK == 0 else L
    H = RET_HEADS
    log_g = jnp.log1p(-jnp.exp2(-5.0 - jnp.arange(H, dtype=F32)))
    idx = jnp.arange(c, dtype=F32)
    diff = idx[:, None] - idx[None, :]
    causal = diff >= 0
    intra = jnp.where(causal[None], jnp.exp(jnp.where(causal, diff, 0.0)[None] * log_g[:, None, None]), 0.0)
    q_decay = jnp.exp((idx[None, :] + 1.0) * log_g[:, None])[:, :, None]
    k_decay = jnp.exp((c - 1.0 - idx)[None, :] * log_g[:, None])[:, :, None]
    s_decay = jnp.exp(c * log_g)
    blk = lambda col: pl.BlockSpec((1, c, RET_WIDTH), lambda b, n, col=col: (b, n, col))
    table = pl.BlockSpec((c, RET_DK // 2), lambda b, n: (n, 0))
    fixed3 = lambda shape: pl.BlockSpec(shape, lambda b, n: (0, 0, 0))
    state = pl.BlockSpec((1, H, RET_DK, RET_DK), lambda b, n: (b, 0, 0, 0))
    return pl.pallas_call(
        _retention_kernel,
        out_shape=(jax.ShapeDtypeStruct((B, L, RET_WIDTH), BF16), jax.ShapeDtypeStruct(s0.shape, F32)),
        grid=(B, L // c),
        in_specs=[pl.BlockSpec(memory_space=pltpu.SMEM)] + [blk(col) for col in cols]
                 + [table, table, fixed3((H, c, c)), fixed3((H, c, 1)), fixed3((H, c, 1)),
                    pl.BlockSpec((H, RET_DK), lambda b, n: (0, 0)), state],
        out_specs=(pl.BlockSpec((1, c, RET_WIDTH), lambda b, n: (b, n, 0)), state),
        scratch_shapes=[pltpu.VMEM((H, RET_DK, RET_DK), F32)],
        compiler_params=_params("parallel", "arbitrary"),
        name="retention",
    )(s_decay, proj, proj, proj, proj, cos, sin, intra, q_decay, k_decay, ret_norm, s0)


ATTN_SCALE = HEAD_DIM ** -0.5


def rope_tables_full(pos):
    cos, sin = rope_tables(pos, HEAD_DIM // 2)
    return jnp.concatenate([cos, cos], axis=-1), jnp.concatenate([-sin, sin], axis=-1)


def _rope_roll(x, cos_full, sin_signed):
    return x * cos_full + pltpu.roll(x, HEAD_DIM // 2, 1) * sin_signed


def _sink_softmax(s, mask, sink):
    s = jnp.where(mask, s, -jnp.inf)
    m = jnp.maximum(jnp.max(s, axis=-1, keepdims=True), sink)
    e = jnp.exp(s - m)
    return e / (jnp.sum(e, axis=-1, keepdims=True) + jnp.exp(sink - m))


def _swa_prompt_kernel(sinks_ref, q_ref, kc_ref, kp_ref, vc_ref, vp_ref, cc_ref, sc_ref, cp_ref, sp_ref,
                       o_ref, krot_ref, *, blk):
    j = pl.program_id(0)
    r = lax.broadcasted_iota(jnp.int32, (blk, 2 * blk), 0)
    c = lax.broadcasted_iota(jnp.int32, (blk, 2 * blk), 1)
    diff = blk + r - c
    mask = ((j - 1) * blk + c >= 0) & (diff >= 0) & (diff < WINDOW)
    cos_c, sin_c, cos_p, sin_p = cc_ref[...], sc_ref[...], cp_ref[...], sp_ref[...]
    for g in range(SWA_KV_HEADS):
        kv = slice(g * HEAD_DIM, (g + 1) * HEAD_DIM)
        k_cur = _rope_roll(kc_ref[:, kv], cos_c, sin_c)
        krot_ref[:, kv] = k_cur
        k_all = jnp.concatenate([_rope_roll(kp_ref[:, kv], cos_p, sin_p), k_cur], axis=0).astype(BF16)
        v_all = jnp.concatenate([vp_ref[:, kv], vc_ref[:, kv]], axis=0).astype(BF16)
        for hh in range(SWA_GROUP):
            h = g * SWA_GROUP + hh
            cols = slice(h * HEAD_DIM, (h + 1) * HEAD_DIM)
            q = _rope_roll(q_ref[:, cols], cos_c, sin_c).astype(BF16)
            p = _sink_softmax(_dot_nt(q, k_all) * ATTN_SCALE, mask, sinks_ref[h])
            o_ref[:, cols] = _dot(p.astype(BF16), v_all).astype(o_ref.dtype)


def swa_prompt(proj, q_col, k_col, v_col, cos_full, sin_signed, sinks, *, L, blk=128):
    assert L % blk == 0 and blk == WINDOW
    cur = lambda j: (j, 0)
    prev = lambda j: (jnp.maximum(j - 1, 0), 0)
    kv = lambda col, prv: pl.BlockSpec((blk, SWA_KV_WIDTH),
                                       (lambda j: (jnp.maximum(j - 1, 0), col)) if prv else (lambda j: (j, col)))
    table = lambda m: pl.BlockSpec((blk, HEAD_DIM), m)
    return pl.pallas_call(
        functools.partial(_swa_prompt_kernel, blk=blk),
        out_shape=(jax.ShapeDtypeStruct((L, SWA_WIDTH), BF16), jax.ShapeDtypeStruct((L, SWA_KV_WIDTH), F32)),
        grid=(L // blk,),
        in_specs=[pl.BlockSpec(memory_space=pltpu.SMEM),
                  pl.BlockSpec((blk, SWA_WIDTH), lambda j: (j, q_col)),
                  kv(k_col, False), kv(k_col, True), kv(v_col, False), kv(v_col, True),
                  table(cur), table(cur), table(prev), table(prev)],
        out_specs=(pl.BlockSpec((blk, SWA_WIDTH), cur), pl.BlockSpec((blk, SWA_KV_WIDTH), cur)),
        compiler_params=_params("parallel"),
        name="swa_prompt",
    )(sinks, proj, proj, proj, proj, proj, cos_full, sin_signed, cos_full, sin_signed)


def _swa_sample_kernel(q_ref, knew_ref, vnew_ref, bk_ref, bv_ref, cq_ref, sq_ref, ck_ref, sk_ref, sink_ref,
                       o_ref, knr_ref, kall_ref, vall_ref, *, n_new, past_len):
    wb = bk_ref.shape[1]
    rows = q_ref.shape[2]
    r = lax.broadcasted_iota(jnp.int32, (rows, 2 * wb), 0)
    c = lax.broadcasted_iota(jnp.int32, (rows, 2 * wb), 1)
    diff = r // SWA_GROUP + wb - c
    mask = (past_len - wb + c >= 0) & (diff >= 0) & (diff < WINDOW) & (c < wb + n_new)
    for b, g in [(b, g) for b in range(q_ref.shape[0]) for g in range(SWA_KV_HEADS)]:
        kv = slice(g * HEAD_DIM, (g + 1) * HEAD_DIM)
        k_new = _rope_roll(knew_ref[b, :, kv], ck_ref[...], sk_ref[...])
        knr_ref[b, :, kv] = k_new
        kall_ref[0:wb, :] = bk_ref[b, :, kv]
        kall_ref[wb:2 * wb, :] = jnp.zeros((wb, HEAD_DIM), F32)
        kall_ref[wb:wb + n_new, :] = k_new
        vall_ref[0:wb, :] = bv_ref[b, :, kv]
        vall_ref[wb:2 * wb, :] = jnp.zeros((wb, HEAD_DIM), F32)
        vall_ref[wb:wb + n_new, :] = vnew_ref[b, :, kv]
        q = _rope_roll(q_ref[b, g], cq_ref[...], sq_ref[...]).astype(BF16)
        p = _sink_softmax(_dot_nt(q, kall_ref[...].astype(BF16)) * ATTN_SCALE, mask, sink_ref[g])
        o_ref[b, g] = _dot(p.astype(BF16), vall_ref[...].astype(BF16)).astype(o_ref.dtype)


def swa_sample(qs, ks, vs, buf_k, buf_v, sinks, *, past_len, tb=8):
    B, n, _ = qs.shape
    wb = buf_k.shape[1]
    assert n <= wb
    rows = n * SWA_GROUP
    pos = past_len + jnp.arange(n)
    cos_k, sin_k = rope_tables_full(pos)
    cos_q, sin_q = jnp.repeat(cos_k, SWA_GROUP, axis=0), jnp.repeat(sin_k, SWA_GROUP, axis=0)
    q_g = qs.reshape(B, n, SWA_KV_HEADS, SWA_GROUP, HEAD_DIM).transpose(0, 2, 1, 3, 4).reshape(
        B, SWA_KV_HEADS, rows, HEAD_DIM)
    sink_g = jnp.tile(sinks.astype(F32).reshape(SWA_KV_HEADS, 1, SWA_GROUP), (1, n, 1)).reshape(
        SWA_KV_HEADS, rows, 1)
    tb = min(tb, B)
    assert B % tb == 0
    per_b3 = lambda shape: pl.BlockSpec((tb,) + shape, lambda b: (b, 0, 0))
    per_b4 = pl.BlockSpec((tb, SWA_KV_HEADS, rows, HEAD_DIM), lambda b: (b, 0, 0, 0))
    fixed2 = lambda shape: pl.BlockSpec(shape, lambda b: (0, 0))
    o, k_rot = pl.pallas_call(
        functools.partial(_swa_sample_kernel, n_new=n, past_len=past_len),
        out_shape=(jax.ShapeDtypeStruct((B, SWA_KV_HEADS, rows, HEAD_DIM), BF16),
                   jax.ShapeDtypeStruct((B, n, SWA_KV_WIDTH), F32)),
        grid=(B // tb,),
        in_specs=[per_b4, per_b3((n, SWA_KV_WIDTH)), per_b3((n, SWA_KV_WIDTH)),
                  per_b3((wb, SWA_KV_WIDTH)), per_b3((wb, SWA_KV_WIDTH)),
                  fixed2((rows, HEAD_DIM)), fixed2((rows, HEAD_DIM)), fixed2((n, HEAD_DIM)), fixed2((n, HEAD_DIM)),
                  pl.BlockSpec((SWA_KV_HEADS, rows, 1), lambda b: (0, 0, 0))],
        out_specs=(per_b4, per_b3((n, SWA_KV_WIDTH))),
        scratch_shapes=[pltpu.VMEM((2 * wb, HEAD_DIM), F32), pltpu.VMEM((2 * wb, HEAD_DIM), F32)],
        compiler_params=_params("parallel"),
        name="swa_sample",
    )(q_g, ks, vs, buf_k, buf_v, cos_q, sin_q, cos_k, sin_k, sink_g)
    d_out = o.reshape(B, SWA_KV_HEADS, n, SWA_GROUP, HEAD_DIM).transpose(0, 2, 1, 3, 4).reshape(B, n, SWA_WIDTH)
    return d_out, k_rot


def _xattn_kernel(q_ref, k_ref, v_ref, o_ref):
    for b in range(q_ref.shape[0]):
        for h in range(MEM_HEADS):
            cols = slice(h * HEAD_DIM, (h + 1) * HEAD_DIM)
            s = _dot_nt(q_ref[b, :, cols].astype(BF16), k_ref[b, :, cols].astype(BF16)) * ATTN_SCALE
            e = jnp.exp(s - jnp.max(s, axis=-1, keepdims=True))
            p = e / jnp.sum(e, axis=-1, keepdims=True)
            o_ref[b, :, cols] = _dot(p.astype(BF16), v_ref[b, :, cols].astype(BF16)).astype(o_ref.dtype)


def cross_attention(q, mem_k, mem_v, *, L, tq, tb=1):
    B = q.shape[0]
    n_mem = mem_k.shape[1]
    assert B % tb == 0
    mem = pl.BlockSpec((tb, n_mem, MEM_WIDTH), lambda b, i: (b, 0, 0))
    rows = pl.BlockSpec((tb, tq, MEM_WIDTH), lambda b, i: (b, i, 0))
    return pl.pallas_call(
        _xattn_kernel,
        out_shape=jax.ShapeDtypeStruct((B, L, MEM_WIDTH), BF16),
        grid=(B // tb, L // tq),
        in_specs=[rows, mem, mem],
        out_specs=rows,
        compiler_params=_params("parallel", "parallel"),
        name="cross_attention",
    )(q, mem_k, mem_v)


ROW_TILE = 512
COL_TILE = 512
WIDE_COL_TILE = 1536
LANES = 128
XATTN_PAD = 8
XATTN_SEQS = 8


def _wide_col_tile(n):
    return max(t for t in range(LANES, min(n, WIDE_COL_TILE) + 1, LANES) if n % t == 0)


def kernel(x_prompt, x_sample, cache_sb_k, cache_sb_v, state_lru_h, state_lru_conv, state_ret, cache_swa_k, cache_swa_v, cache_mem_k, cache_mem_v, page_table, mem_prompt, ffn1_norm, ffn1_wg, ffn1_wu, ffn1_wd, mix_norm, ab_w_in, ab_w_out, sb_bias, lru_conv_w, lru_conv_b, lru_wa, lru_ba, lru_wx, lru_bx, lru_lambda, cd_w_in, cd_w_out, ret_norm, swa_sinks, xattn_norm, mem_norm, xattn_wq, xattn_wk, xattn_wv, xattn_wo, ffn2_norm, ffn2_wg, ffn2_wu, ffn2_wd, final_norm):
    Bp, Lp, D = x_prompt.shape
    Bs, Ls, _ = x_sample.shape
    assert Bp == 1
    Ms = Bs * Ls
    depth = ffn1_norm.shape[0]
    n_mem = mem_prompt.shape[1]
    past_len = page_table.shape[1] * PAGE_SIZE
    wb = cache_swa_k.shape[1]
    bf = lambda w: w.astype(BF16)
    lru_p = dict(lru_conv_w=lru_conv_w, lru_conv_b=lru_conv_b, lru_wa=lru_wa, lru_wx=lru_wx, lru_ba=lru_ba,
                 lru_bx=lru_bx, lru_lambda=lru_lambda)
    dense_in = lambda x, g, w: norm_matmul(x, g, w, tm=ROW_TILE, tn=_wide_col_tile(w.shape[1]))
    dense_ffn = functools.partial(ffn, tm=ROW_TILE, tf=COL_TILE)
    dense_out = functools.partial(matmul_residual, tm=ROW_TILE, tn=2 * COL_TILE)
    tmajor = lambda t: jnp.transpose(t, (1, 0, 2))

    mem_kv = [norm_matmul(mem_prompt[0], mem_norm[l], bf(jnp.concatenate([xattn_wk[l], xattn_wv[l]], axis=1)),
                          tm=n_mem, tn=COL_TILE) for l in range(depth)]
    mem_k_p = jnp.stack([kv[:, :MEM_WIDTH] for kv in mem_kv])[:, None]
    mem_v_p = jnp.stack([kv[:, MEM_WIDTH:] for kv in mem_kv])[:, None]

    x = jnp.concatenate([x_prompt.reshape(Lp, D), x_sample.reshape(Ms, D)], axis=0)
    for layer in range(depth):
        x = dense_ffn(x, ffn1_norm[layer], bf(ffn1_wg[layer]), bf(ffn1_wu[layer]), bf(ffn1_wd[layer]))
        if layer % 2 == 0:
            proj = dense_in(x, mix_norm[layer], bf(ab_w_in))
            sproj = proj[Lp:].reshape(Bs, Ls, -1)
            q_s, k_s, v_s = [sproj[..., i * SB_WIDTH:(i + 1) * SB_WIDTH] for i in range(3)]
            a_p = sb_prompt(proj, sb_bias, L=Lp)
            a_s = sb_sample(q_s, k_s, v_s, cache_sb_k, cache_sb_v, page_table, sb_bias)
            b_p, lru_h_p, lru_c_p = lru_prompt(proj, 3, 4, jnp.zeros((CONV_W - 1, LRU_WIDTH), F32),
                                               jnp.zeros((1, LRU_WIDTH), F32), lru_p, L=Lp)
            b_s, lru_h_s, lru_c_s = lru_sample(tmajor(sproj[..., 3 * SB_WIDTH:3 * SB_WIDTH + LRU_WIDTH]),
                                               tmajor(sproj[..., 3 * SB_WIDTH + LRU_WIDTH:]),
                                               tmajor(state_lru_conv), state_lru_h, lru_p)
            mix_in = jnp.concatenate([jnp.concatenate([a_p, b_p], axis=1),
                                      jnp.concatenate([a_s.reshape(Ms, SB_WIDTH),
                                                       tmajor(b_s).reshape(Ms, LRU_WIDTH)], axis=1)], axis=0)
            x = dense_out(x, mix_in, bf(ab_w_out))
            sb_k_p = proj[:Lp, SB_WIDTH:2 * SB_WIDTH].reshape(1, Lp, SB_HEADS, HEAD_DIM)
            sb_v_p = proj[:Lp, 2 * SB_WIDTH:3 * SB_WIDTH].reshape(1, Lp, SB_HEADS, HEAD_DIM)
            sb_k_s = k_s.reshape(Bs, Ls, SB_HEADS, HEAD_DIM)
            sb_v_s = v_s.reshape(Bs, Ls, SB_HEADS, HEAD_DIM)
            lru_c_p = lru_c_p[None]
            lru_c_s = tmajor(lru_c_s)
        else:
            proj = dense_in(x, mix_norm[layer], bf(cd_w_in))
            sproj = proj[Lp:].reshape(Bs, Ls, -1)
            pos_p, pos_s = jnp.arange(Lp), past_len + jnp.arange(Ls)
            c_p, ret_p = retention(proj[None], (0, 1, 2, 3), *rope_tables(pos_p, RET_DK // 2),
                                   jnp.zeros((1, RET_HEADS, RET_DK, RET_DK), F32), ret_norm, L=Lp)
            c_s, ret_s = retention(sproj, (0, 1, 2, 3), *rope_tables(pos_s, RET_DK // 2), state_ret, ret_norm, L=Ls)
            kv0 = 4 * RET_WIDTH + SWA_WIDTH
            d_p, k_rot_p = swa_prompt(proj, 4 * RET_WIDTH // SWA_WIDTH, kv0 // SWA_KV_WIDTH, kv0 // SWA_KV_WIDTH + 1,
                                      *rope_tables_full(pos_p), swa_sinks, L=Lp)
            d_s, k_rot_s = swa_sample(sproj[..., 4 * RET_WIDTH:kv0], sproj[..., kv0:kv0 + SWA_KV_WIDTH],
                                      sproj[..., kv0 + SWA_KV_WIDTH:], cache_swa_k.reshape(Bs, wb, SWA_KV_WIDTH),
                                      cache_swa_v.reshape(Bs, wb, SWA_KV_WIDTH), swa_sinks, past_len=past_len)
            mix_in = jnp.concatenate([jnp.concatenate([c_p[0], d_p], axis=1),
                                      jnp.concatenate([c_s.reshape(Ms, RET_WIDTH),
                                                       d_s.reshape(Ms, SWA_WIDTH)], axis=1)], axis=0)
            x = dense_out(x, mix_in, bf(cd_w_out))
            wp = min(WINDOW, Lp)
            swa_k_p = k_rot_p[Lp - wp:].reshape(1, wp, SWA_KV_HEADS, HEAD_DIM)
            swa_v_p = proj[Lp - wp:Lp, kv0 + SWA_KV_WIDTH:].reshape(1, wp, SWA_KV_HEADS, HEAD_DIM)
            kv_heads = lambda t: t.reshape(Bs, Ls, SWA_KV_HEADS, HEAD_DIM)
            swa_k_s = jnp.concatenate([cache_swa_k, kv_heads(k_rot_s)], axis=1)[:, Ls:]
            swa_v_s = jnp.concatenate([cache_swa_v, kv_heads(sproj[..., kv0 + SWA_KV_WIDTH:])], axis=1)[:, Ls:]
        qx = dense_in(x, xattn_norm[layer], bf(xattn_wq[layer]))
        o_p = cross_attention(qx[None], mem_k_p[layer], mem_v_p[layer], L=Lp, tq=ROW_TILE)
        q_s = jnp.pad(qx[Lp:].reshape(Bs, Ls, MEM_WIDTH), ((0, 0), (0, XATTN_PAD - Ls), (0, 0)))
        o_s = cross_attention(q_s, cache_mem_k[layer].reshape(Bs, n_mem, MEM_WIDTH),
                              cache_mem_v[layer].reshape(Bs, n_mem, MEM_WIDTH), L=XATTN_PAD, tq=XATTN_PAD,
                              tb=XATTN_SEQS)
        x = dense_out(x, jnp.concatenate([o_p[0], o_s[:, :Ls].reshape(Ms, MEM_WIDTH)], axis=0), bf(xattn_wo[layer]))
        x = dense_ffn(x, ffn2_norm[layer], bf(ffn2_wg[layer]), bf(ffn2_wu[layer]), bf(ffn2_wd[layer]))
    y = rmsnorm(x, final_norm, tm=ROW_TILE)
    mem_heads = lambda t: t.reshape(depth, 1, n_mem, MEM_HEADS, HEAD_DIM)
    return (y[:Lp].reshape(1, Lp, D), y[Lp:].reshape(Bs, Ls, D), sb_k_p, sb_v_p, sb_k_s, sb_v_s,
            lru_h_p, lru_h_s, lru_c_p, lru_c_s, ret_p, ret_s, swa_k_p, swa_v_p, swa_k_s, swa_v_s,
            mem_heads(mem_k_p), mem_heads(mem_v_p))
```

```python
import functools

import jax
import jax.numpy as jnp
import numpy as np
from jax import lax
from jax.experimental import pallas as pl
from jax.experimental.pallas import tpu as pltpu

F32 = jnp.float32
BF16 = jnp.bfloat16

HEAD_DIM = 128
SB_HEADS = 8
SB_WIDTH = SB_HEADS * HEAD_DIM
LRU_WIDTH = 1024
LRU_BLOCKS = 8
LRU_BLOCK = LRU_WIDTH // LRU_BLOCKS
CONV_W = 4
RG_C = 8.0
RET_HEADS = 4
RET_DK = 256
RET_WIDTH = RET_HEADS * RET_DK
RET_CHUNK = 128
SWA_HEADS = 8
SWA_KV_HEADS = 2
SWA_GROUP = SWA_HEADS // SWA_KV_HEADS
SWA_WIDTH = SWA_HEADS * HEAD_DIM
SWA_KV_WIDTH = SWA_KV_HEADS * HEAD_DIM
WINDOW = 128
MEM_HEADS = 4
MEM_WIDTH = MEM_HEADS * HEAD_DIM
PAGE_SIZE = 128
ROPE_THETA = 10000.0
EPS = 1e-6

VMEM_LIMIT_V7X = 56 * 1024 * 1024


def _params(*semantics):
    return pltpu.CompilerParams(dimension_semantics=semantics, vmem_limit_bytes=VMEM_LIMIT_V7X)


def _dot(a, b):
    return jnp.dot(a, b, preferred_element_type=F32)


def _dot_nt(a, b):
    return lax.dot_general(a, b, (((1,), (1,)), ((), ())), preferred_element_type=F32)


def _rms(x, g):
    return x * lax.rsqrt(jnp.mean(x * x, axis=-1, keepdims=True) + EPS) * g


def _softplus(x):
    return jnp.maximum(x, 0.0) + jnp.log1p(jnp.exp(-jnp.abs(x)))


def _expm1(x):
    u = jnp.exp(x)
    one = u == 1.0
    return jnp.where(one, x, (u - 1.0) * x / jnp.where(one, 1.0, jnp.log(u)))


def _norm_matmul_kernel(x_ref, g_ref, w_ref, o_ref, h_ref):
    @pl.when(pl.program_id(1) == 0)
    def _():
        h_ref[...] = _rms(x_ref[...], g_ref[...]).astype(BF16)

    o_ref[...] = _dot(h_ref[...], w_ref[...]).astype(o_ref.dtype)


def norm_matmul(x, g, w, *, tm, tn, out_dtype=F32):
    M, D = x.shape
    N = w.shape[1]
    return pl.pallas_call(
        _norm_matmul_kernel,
        out_shape=jax.ShapeDtypeStruct((M, N), out_dtype),
        grid=(M // tm, N // tn),
        in_specs=[pl.BlockSpec((tm, D), lambda i, j: (i, 0)),
                  pl.BlockSpec((1, D), lambda i, j: (0, 0)),
                  pl.BlockSpec((D, tn), lambda i, j: (0, j))],
        out_specs=pl.BlockSpec((tm, tn), lambda i, j: (i, j)),
        scratch_shapes=[pltpu.VMEM((tm, D), BF16)],
        compiler_params=_params("parallel", "arbitrary"),
        name="norm_matmul",
    )(x, g.reshape(1, D), w)


def _ffn_kernel(x_ref, g_ref, wg_ref, wu_ref, wd_ref, o_ref, h_ref, acc_ref):
    f = pl.program_id(1)

    @pl.when(f == 0)
    def _():
        h_ref[...] = _rms(x_ref[...], g_ref[...]).astype(BF16)
        acc_ref[...] = jnp.zeros_like(acc_ref)

    h = h_ref[...]
    a = _dot(h, wg_ref[...])
    b = _dot(h, wu_ref[...])
    t = (a * jax.nn.sigmoid(a) * b).astype(BF16)
    acc_ref[...] += _dot(t, wd_ref[...])

    @pl.when(f == pl.num_programs(1) - 1)
    def _():
        o_ref[...] = x_ref[...] + 0.5 * acc_ref[...]


def ffn(x, g, wg, wu, wd, layer, *, tm, tf):
    M, D = x.shape
    F = wg.shape[2]
    return pl.pallas_call(
        _ffn_kernel,
        out_shape=jax.ShapeDtypeStruct((M, D), F32),
        grid=(M // tm, F // tf),
        in_specs=[pl.BlockSpec((tm, D), lambda i, f: (i, 0)),
                  pl.BlockSpec((1, D), lambda i, f: (0, 0)),
                  pl.BlockSpec((None, D, tf), lambda i, f: (layer, 0, f)),
                  pl.BlockSpec((None, D, tf), lambda i, f: (layer, 0, f)),
                  pl.BlockSpec((None, tf, D), lambda i, f: (layer, f, 0))],
        out_specs=pl.BlockSpec((tm, D), lambda i, f: (i, 0)),
        scratch_shapes=[pltpu.VMEM((tm, D), BF16), pltpu.VMEM((tm, D), F32)],
        compiler_params=_params("parallel", "arbitrary"),
        name="ffn",
    )(x, g.reshape(1, D), wg, wu, wd)


def _matmul_residual_kernel(x_ref, *refs, n_parts, prompt_tiles):
    part_refs, (w_ref, o_ref) = refs[:2 * n_parts], refs[2 * n_parts:]

    def emit(group):
        acc = x_ref[...]
        row = 0
        for p in range(n_parts):
            a = part_refs[2 * p + group][...]
            acc = acc + _dot(a.astype(BF16), w_ref[row:row + a.shape[1], :])
            row += a.shape[1]
        o_ref[...] = acc

    i = pl.program_id(0)
    pl.when(i < prompt_tiles)(lambda: emit(0))
    pl.when(i >= prompt_tiles)(lambda: emit(1))


def matmul_residual(x, parts, w, *, tm, tn):
    M, N = x.shape
    Lp = parts[0][0].shape[0]
    assert Lp % tm == 0 and (M - Lp) % tm == 0 and all(s.shape[0] == M - Lp for _, s in parts)
    prompt_tiles = Lp // tm
    specs, args = [], []
    for prompt, sample in parts:
        k = prompt.shape[1]
        specs += [pl.BlockSpec((tm, k), lambda i, j: (jnp.minimum(i, prompt_tiles - 1), 0)),
                  pl.BlockSpec((tm, k), lambda i, j: (jnp.maximum(i - prompt_tiles, 0), 0))]
        args += [prompt, sample]
    return pl.pallas_call(
        functools.partial(_matmul_residual_kernel, n_parts=len(parts), prompt_tiles=prompt_tiles),
        out_shape=jax.ShapeDtypeStruct((M, N), F32),
        grid=(M // tm, N // tn),
        in_specs=[pl.BlockSpec((tm, tn), lambda i, j: (i, j))] + specs
                 + [pl.BlockSpec((w.shape[0], tn), lambda i, j: (0, j))],
        out_specs=pl.BlockSpec((tm, tn), lambda i, j: (i, j)),
        compiler_params=_params("parallel", "parallel"),
        name="matmul_residual",
    )(x, *args, w)


def _rmsnorm_kernel(x_ref, g_ref, o_ref):
    o_ref[...] = _rms(x_ref[...], g_ref[...])


def rmsnorm(x, g, *, tm):
    M, D = x.shape
    return pl.pallas_call(
        _rmsnorm_kernel,
        out_shape=jax.ShapeDtypeStruct((M, D), F32),
        grid=(M // tm,),
        in_specs=[pl.BlockSpec((tm, D), lambda i: (i, 0)), pl.BlockSpec((1, D), lambda i: (0, 0))],
        out_specs=pl.BlockSpec((tm, D), lambda i: (i, 0)),
        compiler_params=_params("parallel"),
        name="rmsnorm",
    )(x, g.reshape(1, D))


LOG2E = float(np.log2(np.e))
SB_SCALE2 = HEAD_DIM ** -0.5 * LOG2E


def _split_dot(tri_first, x, tri2):
    hi = x.astype(BF16)
    lo = (x - hi.astype(F32)).astype(BF16)
    if tri_first:
        return _dot(tri2, jnp.concatenate([hi, lo], axis=0))
    return _dot(jnp.concatenate([hi, lo], axis=-1), tri2)


def _sb_scores(logits, mask):
    sp = jnp.log2(1.0 + jnp.exp2(-jnp.abs(logits)))
    log_beta = jnp.minimum(logits, 0.0) - sp
    log_rest = log_beta - logits
    if mask is not None:
        log_rest = jnp.where(mask, log_rest, 0.0)
    return log_beta, log_rest


def _sb_prompt_tiles(q, k, v, bias, tri, carry, mask, n_sub):
    blk = tri.shape[1]
    log_beta, log_rest = _sb_scores(_dot_nt(q, k) * SB_SCALE2 + bias, mask)
    ws = [None] * n_sub
    for s in reversed(range(n_sub)):
        cols = slice(s * blk, (s + 1) * blk)
        after = carry + _split_dot(False, log_rest[:, cols], tri)
        w = jnp.exp2(log_beta[:, cols] + after)
        ws[s] = (w if mask is None else jnp.where(mask, w, 0.0)).astype(BF16)
        carry = carry + jnp.sum(log_rest[:, cols], axis=-1, keepdims=True)
    w_all = ws[0] if n_sub == 1 else jnp.concatenate(ws, axis=-1)
    return _dot(w_all, v), carry


SB_SWEEP = 4


def _sb_prompt_kernel(bias_ref, q_ref, k_ref, v_ref, o_ref, kb_ref, vb_ref, *, blk):
    h = pl.program_id(0)
    i = pl.program_id(1)
    L = k_ref.shape[0]

    @pl.when(i == 0)
    def _():
        def cast(n, c):
            s = pl.multiple_of(n * blk, blk)
            kb_ref[pl.ds(s, blk), :] = k_ref[pl.ds(s, blk), :].astype(BF16)
            vb_ref[pl.ds(s, blk), :] = v_ref[pl.ds(s, blk), :].astype(BF16)
            return c
        lax.fori_loop(0, L // blk, cast, 0)

    q = q_ref[...].astype(BF16)
    bias = bias_ref[h] * LOG2E
    row = lax.broadcasted_iota(jnp.int32, (blk, blk), 0)
    col = lax.broadcasted_iota(jnp.int32, (blk, blk), 1)
    tri = jnp.where(row > col, 1.0, 0.0).astype(BF16)
    tri = jnp.concatenate([tri, tri], axis=0)

    def sweep(first_block, n_sub, mask, state):
        acc, carry = state
        s = pl.multiple_of(first_block * blk, blk)
        a, carry = _sb_prompt_tiles(q, kb_ref[pl.ds(s, n_sub * blk), :], vb_ref[pl.ds(s, n_sub * blk), :], bias, tri,
                                    carry, mask, n_sub)
        return acc + a, carry

    state = sweep(i, 1, col < row, (jnp.zeros((blk, HEAD_DIM), F32), jnp.zeros((blk, 1), F32)))
    odd = i % SB_SWEEP
    state = lax.fori_loop(0, odd, lambda n, st: sweep(i - 1 - n, 1, None, st), state)
    state = lax.fori_loop(0, i // SB_SWEEP, lambda n, st: sweep(i - odd - SB_SWEEP * (n + 1), SB_SWEEP, None, st),
                          state)
    o_ref[...] = state[0].astype(o_ref.dtype)


def sb_prompt(qkv, bias, *, L, blk=256):
    blk = min(blk, L)
    assert L % blk == 0
    kern = functools.partial(_sb_prompt_kernel, blk=blk)
    return pl.pallas_call(
        kern,
        out_shape=jax.ShapeDtypeStruct((L, SB_WIDTH), BF16),
        grid=(SB_HEADS, L // blk),
        in_specs=[pl.BlockSpec(memory_space=pltpu.SMEM),
                  pl.BlockSpec((blk, HEAD_DIM), lambda h, i: (i, h)),
                  pl.BlockSpec((L, HEAD_DIM), lambda h, i: (0, SB_HEADS + h)),
                  pl.BlockSpec((L, HEAD_DIM), lambda h, i: (0, 2 * SB_HEADS + h))],
        out_specs=pl.BlockSpec((blk, HEAD_DIM), lambda h, i: (i, h)),
        scratch_shapes=[pltpu.VMEM((L, HEAD_DIM), BF16), pltpu.VMEM((L, HEAD_DIM), BF16)],
        compiler_params=_params("arbitrary", "arbitrary"),
        name="sb_prompt",
    )(bias, qkv, qkv, qkv)


SB_LANES = 128
SB_PACK = 4
SB_STEP = 16


def _page_rows(ref):
    keys = ref.shape[1] // SB_HEADS
    return jnp.concatenate([ref[0, pl.ds(h, keys, stride=SB_HEADS), :] for h in range(SB_HEADS)], axis=-1)


def _fold_lanes(t, width):
    out = t
    for j in range(1, SB_LANES // width):
        out = out + pltpu.roll(t, j * width, 1)
    return out


def _sb_sample_tile(ks, vs, qbd_ref, bias_row, tri_t, carry, mask, width):
    logits = None
    for i, k in enumerate(ks):
        part = _dot(k.astype(BF16), qbd_ref[i])
        logits = part if logits is None else logits + part
    log_beta, log_rest = _sb_scores(logits * SB_SCALE2 + bias_row, mask)
    tot = jnp.sum(log_rest, axis=0, keepdims=True)
    lane = lax.broadcasted_iota(jnp.int32, tot.shape, 1)
    within = jnp.zeros_like(tot)
    for j in range(1, len(ks)):
        within = within + jnp.where(lane >= j * width, pltpu.roll(tot, j * width, 1), 0.0)
    after = carry + within + _split_dot(True, log_rest, tri_t)
    w = jnp.exp2(log_beta + after)
    if mask is not None:
        w = jnp.where(mask, w, 0.0)
    wt = jnp.transpose(w).astype(BF16)
    out = None
    for i, v in enumerate(vs):
        part = _dot(wt[i * width:(i + 1) * width], v.astype(BF16))
        out = part if out is None else out + part
    return out, carry + _fold_lanes(tot, width)


def _sb_sample_kernel(pt_ref, qbd_ref, bias_ref, knew_ref, vnew_ref, *refs, n_q):
    kp, vp = refs[:SB_STEP], refs[SB_STEP:2 * SB_STEP]
    o_ref, acc_ref, carry_ref, qbd4_ref = refs[2 * SB_STEP:]
    s = pl.program_id(1)
    keys = knew_ref.shape[1]
    width = SB_HEADS * n_q
    row = lax.broadcasted_iota(jnp.int32, (keys, SB_LANES), 0)
    col = lax.broadcasted_iota(jnp.int32, (keys, SB_LANES), 1)
    tri_t = jnp.where(col > row, 1.0, 0.0).astype(BF16)
    tri_t = jnp.concatenate([tri_t, tri_t], axis=1)
    bias_row = bias_ref[...] * LOG2E

    @pl.when(s == 0)
    def _():
        q = qbd_ref[0]
        for i in range(SB_PACK):
            qbd4_ref[i] = (q if i == 0 else pltpu.roll(q, i * width, 1)).astype(BF16)
        mask = (row < col % n_q) & (col < width)
        a, c = _sb_sample_tile([knew_ref[0]], [vnew_ref[0]], qbd4_ref, bias_row, tri_t,
                               jnp.zeros((1, SB_LANES), F32), mask, width)
        acc_ref[...] = a
        carry_ref[...] = c

    carry = carry_ref[...]
    acc = acc_ref[...]
    for g in range(SB_STEP // SB_PACK):
        sel = range(g * SB_PACK, (g + 1) * SB_PACK)
        a, carry = _sb_sample_tile([_page_rows(kp[j]) for j in sel], [_page_rows(vp[j]) for j in sel], qbd4_ref,
                                   bias_row, tri_t, carry, None, width)
        acc = acc + a
    acc_ref[...] = acc
    carry_ref[...] = carry

    @pl.when(s == pl.num_programs(1) - 1)
    def _():
        for h in range(SB_HEADS):
            o_ref[0, :, h * HEAD_DIM:(h + 1) * HEAD_DIM] = (
                acc[h * n_q:(h + 1) * n_q, h * HEAD_DIM:(h + 1) * HEAD_DIM].astype(o_ref.dtype))


def sb_sample(q, k, v, pool_k, pool_v, page_table, bias):
    B, n_q, _ = q.shape
    n_pool = pool_k.shape[0]
    n_pages = page_table.shape[1]
    width = SB_HEADS * n_q
    assert width * SB_PACK == SB_LANES and n_q <= PAGE_SIZE and n_pages % SB_STEP == 0 and SB_STEP % SB_PACK == 0
    eye = jnp.eye(SB_HEADS, dtype=F32)
    qh = q.reshape(B, n_q, SB_HEADS, HEAD_DIM)
    qbd = jnp.einsum('bthd,hg->bhdgt', qh, eye).reshape(B, SB_WIDTH, width)
    qbd = jnp.pad(qbd, ((0, 0), (0, 0), (0, SB_LANES - width)))
    bias_row = jnp.tile(jnp.repeat(bias.astype(F32), n_q), SB_PACK).reshape(1, SB_LANES)
    knew = jnp.pad(k, ((0, 0), (0, PAGE_SIZE - n_q), (0, 0)))
    vnew = jnp.pad(v, ((0, 0), (0, PAGE_SIZE - n_q), (0, 0)))
    pool_k = pool_k.reshape(n_pool, PAGE_SIZE * SB_HEADS, HEAD_DIM)
    pool_v = pool_v.reshape(n_pool, PAGE_SIZE * SB_HEADS, HEAD_DIM)

    def page_spec(j):
        return pl.BlockSpec((1, PAGE_SIZE * SB_HEADS, HEAD_DIM),
                            lambda b, s, pt: (pt[b * n_pages + n_pages - 1 - s * SB_STEP - j], 0, 0))

    per_seq = lambda b, s, pt: (b, 0, 0)
    grid_spec = pltpu.PrefetchScalarGridSpec(
        num_scalar_prefetch=1,
        grid=(B, n_pages // SB_STEP),
        in_specs=[pl.BlockSpec((1, SB_WIDTH, SB_LANES), per_seq),
                  pl.BlockSpec((1, SB_LANES), lambda b, s, pt: (0, 0)),
                  pl.BlockSpec((1, PAGE_SIZE, SB_WIDTH), per_seq),
                  pl.BlockSpec((1, PAGE_SIZE, SB_WIDTH), per_seq)]
                 + [page_spec(j) for j in range(SB_STEP)] * 2,
        out_specs=pl.BlockSpec((1, n_q, SB_WIDTH), per_seq),
        scratch_shapes=[pltpu.VMEM((width, SB_WIDTH), F32), pltpu.VMEM((1, SB_LANES), F32),
                        pltpu.VMEM((SB_PACK, SB_WIDTH, SB_LANES), BF16)],
    )
    return pl.pallas_call(
        functools.partial(_sb_sample_kernel, n_q=n_q),
        out_shape=jax.ShapeDtypeStruct((B, n_q, SB_WIDTH), BF16),
        grid_spec=grid_spec,
        compiler_params=_params("parallel", "arbitrary"),
        name="sb_sample",
    )(page_table.reshape(-1), qbd, bias_row, knew, vnew, *([pool_k] * SB_STEP), *([pool_v] * SB_STEP))


def _lru_coeffs(y, wa_ref, wx_ref, ba, bx, sp_lam):
    yb = y.astype(BF16)
    r_parts, i_parts = [], []
    for n in range(LRU_BLOCKS):
        yn = yb[:, n * LRU_BLOCK:(n + 1) * LRU_BLOCK]
        r_parts.append(_dot(yn, wa_ref[n]))
        i_parts.append(_dot(yn, wx_ref[n]))
    r = jax.nn.sigmoid(jnp.concatenate(r_parts, axis=-1) + ba)
    i = jax.nn.sigmoid(jnp.concatenate(i_parts, axis=-1) + bx)
    log_a = -RG_C * r * sp_lam
    return jnp.exp(log_a), jnp.sqrt(-_expm1(2.0 * log_a)) * (i * y)


LRU_PAD = 8


def _lru_prompt_kernel(gate_ref, xb_ref, conv0_ref, h0_ref, cw_ref, cb_ref, wa_ref, wx_ref, ba_ref, bx_ref,
                       lam_ref, o_ref, hlast_ref, cnew_ref, ext_ref, h_ref, a_ref, b_ref, hs_ref, *, T):
    i = pl.program_id(0)
    tail = CONV_W - 1

    @pl.when(i == 0)
    def _():
        ext_ref[LRU_PAD - tail:LRU_PAD, :] = conv0_ref[...]
        h_ref[...] = h0_ref[...]

    ext_ref[LRU_PAD:LRU_PAD + T, :] = xb_ref[...]
    y = cb_ref[...]
    for j in range(CONV_W):
        y = y + cw_ref[j:j + 1, :] * ext_ref[LRU_PAD - tail + j:LRU_PAD - tail + j + T, :]
    a, b = _lru_coeffs(y, wa_ref, wx_ref, ba_ref[...], bx_ref[...], _softplus(-lam_ref[...]))
    a_ref[...] = a
    b_ref[...] = b

    def step(t, h):
        h = a_ref[pl.ds(t, 1), :] * h + b_ref[pl.ds(t, 1), :]
        hs_ref[pl.ds(t, 1), :] = h
        return h

    h = lax.fori_loop(0, T, step, h_ref[...], unroll=8)
    h_ref[...] = h
    o_ref[...] = (jax.nn.gelu(gate_ref[...]) * hs_ref[...]).astype(o_ref.dtype)
    last_rows = ext_ref[LRU_PAD + T - tail:LRU_PAD + T, :]
    ext_ref[LRU_PAD - tail:LRU_PAD, :] = last_rows

    @pl.when(i == pl.num_programs(0) - 1)
    def _():
        hlast_ref[...] = h
        cnew_ref[...] = last_rows


def _lru_weights(p):
    row = lambda t: t.reshape(1, LRU_WIDTH)
    return (p['lru_conv_w'], row(p['lru_conv_b']), p['lru_wa'].astype(BF16), p['lru_wx'].astype(BF16),
            row(p['lru_ba']), row(p['lru_bx']), row(p['lru_lambda']))


def _lru_weight_specs():
    zero2 = lambda i: (0, 0)
    zero3 = lambda i: (0, 0, 0)
    vec = pl.BlockSpec((1, LRU_WIDTH), zero2)
    mat = pl.BlockSpec((LRU_BLOCKS, LRU_BLOCK, LRU_BLOCK), zero3)
    return [pl.BlockSpec((CONV_W, LRU_WIDTH), zero2), vec, mat, mat, vec, vec, vec]


def lru_prompt(proj, gate_col, xb_col, conv0, h0, p, *, L, T=256):
    T = min(T, L)
    assert L % T == 0 and T >= CONV_W - 1
    W = LRU_WIDTH
    fixed = lambda i: (0, 0)
    return pl.pallas_call(
        functools.partial(_lru_prompt_kernel, T=T),
        out_shape=(jax.ShapeDtypeStruct((L, W), BF16), jax.ShapeDtypeStruct((1, W), F32),
                   jax.ShapeDtypeStruct((CONV_W - 1, W), F32)),
        grid=(L // T,),
        in_specs=[pl.BlockSpec((T, W), lambda i: (i, gate_col)), pl.BlockSpec((T, W), lambda i: (i, xb_col)),
                  pl.BlockSpec((CONV_W - 1, W), fixed), pl.BlockSpec((1, W), fixed)] + _lru_weight_specs(),
        out_specs=(pl.BlockSpec((T, W), lambda i: (i, 0)), pl.BlockSpec((1, W), fixed),
                   pl.BlockSpec((CONV_W - 1, W), fixed)),
        scratch_shapes=[pltpu.VMEM((LRU_PAD + T, W), F32), pltpu.VMEM((1, W), F32), pltpu.VMEM((T, W), F32),
                        pltpu.VMEM((T, W), F32), pltpu.VMEM((T, W), F32)],
        compiler_params=_params("arbitrary"),
        name="lru_prompt",
    )(proj, proj, conv0, h0, *_lru_weights(p))


def _lru_sample_kernel(gate_ref, xb_ref, conv0_ref, h0_ref, cw_ref, cb_ref, wa_ref, wx_ref, ba_ref, bx_ref,
                       lam_ref, o_ref, hlast_ref, cnew_ref, *, L):
    tail = CONV_W - 1
    xs = [conv0_ref[j] for j in range(tail)] + [xb_ref[t] for t in range(L)]
    sp_lam = _softplus(-lam_ref[...])
    h = h0_ref[...]
    for t in range(L):
        y = cb_ref[...]
        for j in range(CONV_W):
            y = y + cw_ref[j:j + 1, :] * xs[t + j]
        a, b = _lru_coeffs(y, wa_ref, wx_ref, ba_ref[...], bx_ref[...], sp_lam)
        h = a * h + b
        o_ref[t] = (jax.nn.gelu(gate_ref[t]) * h).astype(o_ref.dtype)
    hlast_ref[...] = h
    for j in range(tail):
        cnew_ref[j] = xs[L + j]


def lru_sample(gate_t, xb_t, conv0_t, h0, p, *, tb=64):
    L, B, W = xb_t.shape
    tb = min(tb, B)
    assert B % tb == 0
    tmaj = lambda n: pl.BlockSpec((n, tb, W), lambda i: (0, i, 0))
    rows = pl.BlockSpec((tb, W), lambda i: (i, 0))
    return pl.pallas_call(
        functools.partial(_lru_sample_kernel, L=L),
        out_shape=(jax.ShapeDtypeStruct((L, B, W), BF16), jax.ShapeDtypeStruct((B, W), F32),
                   jax.ShapeDtypeStruct((CONV_W - 1, B, W), F32)),
        grid=(B // tb,),
        in_specs=[tmaj(L), tmaj(L), tmaj(CONV_W - 1), rows] + _lru_weight_specs(),
        out_specs=(tmaj(L), rows, tmaj(CONV_W - 1)),
        compiler_params=_params("parallel"),
        name="lru_sample",
    )(gate_t, xb_t, conv0_t, h0, *_lru_weights(p))


def rope_tables(pos, half):
    inv = ROPE_THETA ** (-jnp.arange(half, dtype=F32) / half)
    ang = pos.astype(F32)[:, None] * inv[None, :]
    return jnp.cos(ang), jnp.sin(ang)


def _dot_tn(a, b):
    return lax.dot_general(a, b, (((0,), (0,)), ((), ())), preferred_element_type=F32)


def _rope_halves(x, cos, sin):
    half = x.shape[-1] // 2
    x1, x2 = x[:, :half], x[:, half:]
    return jnp.concatenate([x1 * cos - x2 * sin, x1 * sin + x2 * cos], axis=-1)


def _retention_kernel(sdec_ref, q_ref, k_ref, v_ref, g_ref, cos_ref, sin_ref, intra_ref, qd_ref, kd_ref, norm_ref,
                      s0_ref, o_ref, sout_ref, s_ref):
    n = pl.program_id(1)

    @pl.when(n == 0)
    def _():
        s_ref[...] = s0_ref[0]

    cos, sin = cos_ref[...], sin_ref[...]
    for h in range(RET_HEADS):
        cols = slice(h * RET_DK, (h + 1) * RET_DK)
        q = _rope_halves(q_ref[0, :, cols], cos, sin)
        k = _rope_halves(k_ref[0, :, cols], cos, sin) * (RET_DK ** -0.5)
        v = v_ref[0, :, cols].astype(BF16)
        qb = q.astype(BF16)
        s = s_ref[h]
        att = _dot_nt(qb, k.astype(BF16)) * intra_ref[h]
        o = _dot(att.astype(BF16), v) + _dot(qb, s.astype(BF16)) * qd_ref[h]
        s_ref[h] = sdec_ref[h] * s + _dot_tn((k * kd_ref[h]).astype(BF16), v)
        o = o * lax.rsqrt(jnp.mean(o * o, axis=-1, keepdims=True) + EPS) * norm_ref[h:h + 1, :]
        g = g_ref[0, :, cols]
        o_ref[0, :, cols] = (o * (g * jax.nn.sigmoid(g))).astype(o_ref.dtype)

    @pl.when(n == pl.num_programs(1) - 1)
    def _():
        sout_ref[0] = s_ref[...]


def retention(proj, cols, cos, sin, s0, ret_norm, *, L):
    B = proj.shape[0]
    c = RET_CHUNK if L % RET_CHUNK == 0 else L
    H = RET_HEADS
    log_g = jnp.log1p(-jnp.exp2(-5.0 - jnp.arange(H, dtype=F32)))
    idx = jnp.arange(c, dtype=F32)
    diff = idx[:, None] - idx[None, :]
    causal = diff >= 0
    intra = jnp.where(causal[None], jnp.exp(jnp.where(causal, diff, 0.0)[None] * log_g[:, None, None]), 0.0)
    q_decay = jnp.exp((idx[None, :] + 1.0) * log_g[:, None])[:, :, None]
    k_decay = jnp.exp((c - 1.0 - idx)[None, :] * log_g[:, None])[:, :, None]
    s_decay = jnp.exp(c * log_g)
    blk = lambda col: pl.BlockSpec((1, c, RET_WIDTH), lambda b, n, col=col: (b, n, col))
    table = pl.BlockSpec((c, RET_DK // 2), lambda b, n: (n, 0))
    fixed3 = lambda shape: pl.BlockSpec(shape, lambda b, n: (0, 0, 0))
    state = pl.BlockSpec((1, H, RET_DK, RET_DK), lambda b, n: (b, 0, 0, 0))
    return pl.pallas_call(
        _retention_kernel,
        out_shape=(jax.ShapeDtypeStruct((B, L, RET_WIDTH), BF16), jax.ShapeDtypeStruct(s0.shape, F32)),
        grid=(B, L // c),
        in_specs=[pl.BlockSpec(memory_space=pltpu.SMEM)] + [blk(col) for col in cols]
                 + [table, table, fixed3((H, c, c)), fixed3((H, c, 1)), fixed3((H, c, 1)),
                    pl.BlockSpec((H, RET_DK), lambda b, n: (0, 0)), state],
        out_specs=(pl.BlockSpec((1, c, RET_WIDTH), lambda b, n: (b, n, 0)), state),
        scratch_shapes=[pltpu.VMEM((H, RET_DK, RET_DK), F32)],
        compiler_params=_params("parallel", "arbitrary"),
        name="retention",
    )(s_decay, proj, proj, proj, proj, cos, sin, intra, q_decay, k_decay, ret_norm, s0)


ATTN_SCALE = HEAD_DIM ** -0.5


def rope_tables_full(pos):
    cos, sin = rope_tables(pos, HEAD_DIM // 2)
    return jnp.concatenate([cos, cos], axis=-1), jnp.concatenate([-sin, sin], axis=-1)


def _rope_roll(x, cos_full, sin_signed):
    return x * cos_full + pltpu.roll(x, HEAD_DIM // 2, 1) * sin_signed


def _sink_softmax(s, mask, sink):
    s = jnp.where(mask, s, -jnp.inf)
    m = jnp.maximum(jnp.max(s, axis=-1, keepdims=True), sink)
    e = jnp.exp(s - m)
    return e / (jnp.sum(e, axis=-1, keepdims=True) + jnp.exp(sink - m))


def _swa_prompt_kernel(sinks_ref, q_ref, kc_ref, kp_ref, vc_ref, vp_ref, cc_ref, sc_ref, cp_ref, sp_ref,
                       o_ref, krot_ref, *, blk):
    j = pl.program_id(0)
    r = lax.broadcasted_iota(jnp.int32, (blk, 2 * blk), 0)
    c = lax.broadcasted_iota(jnp.int32, (blk, 2 * blk), 1)
    diff = blk + r - c
    mask = ((j - 1) * blk + c >= 0) & (diff >= 0) & (diff < WINDOW)
    cos_c, sin_c, cos_p, sin_p = cc_ref[...], sc_ref[...], cp_ref[...], sp_ref[...]
    for g in range(SWA_KV_HEADS):
        kv = slice(g * HEAD_DIM, (g + 1) * HEAD_DIM)
        k_cur = _rope_roll(kc_ref[:, kv], cos_c, sin_c)
        krot_ref[:, kv] = k_cur
        k_all = jnp.concatenate([_rope_roll(kp_ref[:, kv], cos_p, sin_p), k_cur], axis=0).astype(BF16)
        v_all = jnp.concatenate([vp_ref[:, kv], vc_ref[:, kv]], axis=0).astype(BF16)
        for hh in range(SWA_GROUP):
            h = g * SWA_GROUP + hh
            cols = slice(h * HEAD_DIM, (h + 1) * HEAD_DIM)
            q = _rope_roll(q_ref[:, cols], cos_c, sin_c).astype(BF16)
            p = _sink_softmax(_dot_nt(q, k_all) * ATTN_SCALE, mask, sinks_ref[h])
            o_ref[:, cols] = _dot(p.astype(BF16), v_all).astype(o_ref.dtype)


def swa_prompt(proj, q_col, k_col, v_col, cos_full, sin_signed, sinks, *, L, blk=128):
    assert L % blk == 0 and blk == WINDOW
    cur = lambda j: (j, 0)
    prev = lambda j: (jnp.maximum(j - 1, 0), 0)
    kv = lambda col, prv: pl.BlockSpec((blk, SWA_KV_WIDTH),
                                       (lambda j: (jnp.maximum(j - 1, 0), col)) if prv else (lambda j: (j, col)))
    table = lambda m: pl.BlockSpec((blk, HEAD_DIM), m)
    return pl.pallas_call(
        functools.partial(_swa_prompt_kernel, blk=blk),
        out_shape=(jax.ShapeDtypeStruct((L, SWA_WIDTH), BF16), jax.ShapeDtypeStruct((L, SWA_KV_WIDTH), F32)),
        grid=(L // blk,),
        in_specs=[pl.BlockSpec(memory_space=pltpu.SMEM),
                  pl.BlockSpec((blk, SWA_WIDTH), lambda j: (j, q_col)),
                  kv(k_col, False), kv(k_col, True), kv(v_col, False), kv(v_col, True),
                  table(cur), table(cur), table(prev), table(prev)],
        out_specs=(pl.BlockSpec((blk, SWA_WIDTH), cur), pl.BlockSpec((blk, SWA_KV_WIDTH), cur)),
        compiler_params=_params("parallel"),
        name="swa_prompt",
    )(sinks, proj, proj, proj, proj, proj, cos_full, sin_signed, cos_full, sin_signed)


def _swa_sample_kernel(q_ref, knew_ref, vnew_ref, bk_ref, bv_ref, cq_ref, sq_ref, ck_ref, sk_ref, sink_ref,
                       o_ref, knr_ref, kall_ref, vall_ref, *, n_new, past_len):
    wb = bk_ref.shape[1]
    rows = q_ref.shape[2]
    r = lax.broadcasted_iota(jnp.int32, (rows, 2 * wb), 0)
    c = lax.broadcasted_iota(jnp.int32, (rows, 2 * wb), 1)
    diff = r // SWA_GROUP + wb - c
    mask = (past_len - wb + c >= 0) & (diff >= 0) & (diff < WINDOW) & (c < wb + n_new)
    for b, g in [(b, g) for b in range(q_ref.shape[0]) for g in range(SWA_KV_HEADS)]:
        kv = slice(g * HEAD_DIM, (g + 1) * HEAD_DIM)
        k_new = _rope_roll(knew_ref[b, :, kv], ck_ref[...], sk_ref[...])
        knr_ref[b, :, kv] = k_new
        kall_ref[0:wb, :] = bk_ref[b, :, kv]
        kall_ref[wb:2 * wb, :] = jnp.zeros((wb, HEAD_DIM), F32)
        kall_ref[wb:wb + n_new, :] = k_new
        vall_ref[0:wb, :] = bv_ref[b, :, kv]
        vall_ref[wb:2 * wb, :] = jnp.zeros((wb, HEAD_DIM), F32)
        vall_ref[wb:wb + n_new, :] = vnew_ref[b, :, kv]
        q = _rope_roll(q_ref[b, g], cq_ref[...], sq_ref[...]).astype(BF16)
        p = _sink_softmax(_dot_nt(q, kall_ref[...].astype(BF16)) * ATTN_SCALE, mask, sink_ref[g])
        o_ref[b, g] = _dot(p.astype(BF16), vall_ref[...].astype(BF16)).astype(o_ref.dtype)


def swa_sample(qs, ks, vs, buf_k, buf_v, sinks, *, past_len, tb=8):
    B, n, _ = qs.shape
    wb = buf_k.shape[1]
    assert n <= wb
    rows = n * SWA_GROUP
    pos = past_len + jnp.arange(n)
    cos_k, sin_k = rope_tables_full(pos)
    cos_q, sin_q = jnp.repeat(cos_k, SWA_GROUP, axis=0), jnp.repeat(sin_k, SWA_GROUP, axis=0)
    q_g = qs.reshape(B, n, SWA_KV_HEADS, SWA_GROUP, HEAD_DIM).transpose(0, 2, 1, 3, 4).reshape(
        B, SWA_KV_HEADS, rows, HEAD_DIM)
    sink_g = jnp.tile(sinks.astype(F32).reshape(SWA_KV_HEADS, 1, SWA_GROUP), (1, n, 1)).reshape(
        SWA_KV_HEADS, rows, 1)
    tb = min(tb, B)
    assert B % tb == 0
    per_b3 = lambda shape: pl.BlockSpec((tb,) + shape, lambda b: (b, 0, 0))
    per_b4 = pl.BlockSpec((tb, SWA_KV_HEADS, rows, HEAD_DIM), lambda b: (b, 0, 0, 0))
    fixed2 = lambda shape: pl.BlockSpec(shape, lambda b: (0, 0))
    o, k_rot = pl.pallas_call(
        functools.partial(_swa_sample_kernel, n_new=n, past_len=past_len),
        out_shape=(jax.ShapeDtypeStruct((B, SWA_KV_HEADS, rows, HEAD_DIM), BF16),
                   jax.ShapeDtypeStruct((B, n, SWA_KV_WIDTH), F32)),
        grid=(B // tb,),
        in_specs=[per_b4, per_b3((n, SWA_KV_WIDTH)), per_b3((n, SWA_KV_WIDTH)),
                  per_b3((wb, SWA_KV_WIDTH)), per_b3((wb, SWA_KV_WIDTH)),
                  fixed2((rows, HEAD_DIM)), fixed2((rows, HEAD_DIM)), fixed2((n, HEAD_DIM)), fixed2((n, HEAD_DIM)),
                  pl.BlockSpec((SWA_KV_HEADS, rows, 1), lambda b: (0, 0, 0))],
        out_specs=(per_b4, per_b3((n, SWA_KV_WIDTH))),
        scratch_shapes=[pltpu.VMEM((2 * wb, HEAD_DIM), F32), pltpu.VMEM((2 * wb, HEAD_DIM), F32)],
        compiler_params=_params("parallel"),
        name="swa_sample",
    )(q_g, ks, vs, buf_k, buf_v, cos_q, sin_q, cos_k, sin_k, sink_g)
    d_out = o.reshape(B, SWA_KV_HEADS, n, SWA_GROUP, HEAD_DIM).transpose(0, 2, 1, 3, 4).reshape(B, n, SWA_WIDTH)
    return d_out, k_rot


def _xattn_kernel(q_ref, k_ref, v_ref, o_ref):
    for b in range(q_ref.shape[0]):
        for h in range(MEM_HEADS):
            cols = slice(h * HEAD_DIM, (h + 1) * HEAD_DIM)
            s = _dot_nt(q_ref[b, :, cols].astype(BF16), k_ref[b, :, cols].astype(BF16)) * ATTN_SCALE
            e = jnp.exp(s - jnp.max(s, axis=-1, keepdims=True))
            p = e / jnp.sum(e, axis=-1, keepdims=True)
            o_ref[b, :, cols] = _dot(p.astype(BF16), v_ref[b, :, cols].astype(BF16)).astype(o_ref.dtype)


def cross_attention(q, mem_k, mem_v, layer, *, L, tq):
    B = q.shape[0]
    n_mem = mem_k.shape[2]
    mem = pl.BlockSpec((None, 1, n_mem, MEM_WIDTH), lambda b, i: (layer, b, 0, 0))
    rows = pl.BlockSpec((1, tq, MEM_WIDTH), lambda b, i: (b, i, 0))
    return pl.pallas_call(
        _xattn_kernel,
        out_shape=jax.ShapeDtypeStruct((B, L, MEM_WIDTH), BF16),
        grid=(B, L // tq),
        in_specs=[rows, mem, mem],
        out_specs=rows,
        compiler_params=_params("parallel", "parallel"),
        name="cross_attention",
    )(q, mem_k, mem_v)


XATTN_LANES = 128


def _xattn_packed_kernel(qbd_ref, k_ref, v_ref, o_ref, *, n_q):
    n_seq = k_ref.shape[0]
    width = MEM_HEADS * n_q
    s = None
    for j in range(n_seq):
        part = _dot(k_ref[j].astype(BF16), qbd_ref[j])
        s = part if s is None else s + part
    s = s * ATTN_SCALE
    e = jnp.exp(s - jnp.max(s, axis=0, keepdims=True))
    p = e / jnp.sum(e, axis=0, keepdims=True)
    pt = jnp.transpose(p).astype(BF16)
    for j in range(n_seq):
        o = _dot(pt[j * width:(j + 1) * width], v_ref[j].astype(BF16))
        for h in range(MEM_HEADS):
            cols = slice(h * HEAD_DIM, (h + 1) * HEAD_DIM)
            o_ref[j, :, cols] = o[h * n_q:(h + 1) * n_q, cols].astype(o_ref.dtype)


def cross_attention_packed(q, mem_k, mem_v, layer):
    B, n_q, _ = q.shape
    n_mem = mem_k.shape[2]
    width = MEM_HEADS * n_q
    n_seq = XATTN_LANES // width
    assert n_seq * width == XATTN_LANES and B % n_seq == 0
    qh = q.reshape(B, n_q, MEM_HEADS, HEAD_DIM)
    place = jnp.eye(n_seq, dtype=F32)[jnp.arange(B) % n_seq]
    qbd = jnp.einsum('bthd,hg,bj->bhdjgt', qh, jnp.eye(MEM_HEADS, dtype=F32), place)
    qbd = qbd.reshape(B, MEM_WIDTH, XATTN_LANES).astype(BF16)
    mem = pl.BlockSpec((None, n_seq, n_mem, MEM_WIDTH), lambda b: (layer, b, 0, 0))
    return pl.pallas_call(
        functools.partial(_xattn_packed_kernel, n_q=n_q),
        out_shape=jax.ShapeDtypeStruct((B, n_q, MEM_WIDTH), BF16),
        grid=(B // n_seq,),
        in_specs=[pl.BlockSpec((n_seq, MEM_WIDTH, XATTN_LANES), lambda b: (b, 0, 0)), mem, mem],
        out_specs=pl.BlockSpec((n_seq, n_q, MEM_WIDTH), lambda b: (b, 0, 0)),
        compiler_params=_params("parallel"),
        name="cross_attention_packed",
    )(qbd, mem_k, mem_v)


ROW_TILE = 512
COL_TILE = 512
WIDE_COL_TILE = 1536
LANES = 128


def _wide_col_tile(n):
    return max(t for t in range(LANES, min(n, WIDE_COL_TILE) + 1, LANES) if n % t == 0)


def kernel(x_prompt, x_sample, cache_sb_k, cache_sb_v, state_lru_h, state_lru_conv, state_ret, cache_swa_k, cache_swa_v, cache_mem_k, cache_mem_v, page_table, mem_prompt, ffn1_norm, ffn1_wg, ffn1_wu, ffn1_wd, mix_norm, ab_w_in, ab_w_out, sb_bias, lru_conv_w, lru_conv_b, lru_wa, lru_ba, lru_wx, lru_bx, lru_lambda, cd_w_in, cd_w_out, ret_norm, swa_sinks, xattn_norm, mem_norm, xattn_wq, xattn_wk, xattn_wv, xattn_wo, ffn2_norm, ffn2_wg, ffn2_wu, ffn2_wd, final_norm):
    Bp, Lp, D = x_prompt.shape
    Bs, Ls, _ = x_sample.shape
    assert Bp == 1
    Ms = Bs * Ls
    depth = ffn1_norm.shape[0]
    n_mem = mem_prompt.shape[1]
    past_len = page_table.shape[1] * PAGE_SIZE
    wb = cache_swa_k.shape[1]
    bf = lambda w: w.astype(BF16)
    lru_p = dict(lru_conv_w=lru_conv_w, lru_conv_b=lru_conv_b, lru_wa=lru_wa, lru_wx=lru_wx, lru_ba=lru_ba,
                 lru_bx=lru_bx, lru_lambda=lru_lambda)
    dense_in = lambda x, g, w: norm_matmul(x, g, w, tm=ROW_TILE, tn=_wide_col_tile(w.shape[1]))
    dense_ffn = functools.partial(ffn, tm=ROW_TILE, tf=COL_TILE)
    dense_out = functools.partial(matmul_residual, tm=ROW_TILE, tn=2 * COL_TILE)
    tmajor = lambda t: jnp.transpose(t, (1, 0, 2))

    mem_kv = [norm_matmul(mem_prompt[0], mem_norm[l], bf(jnp.concatenate([xattn_wk[l], xattn_wv[l]], axis=1)),
                          tm=n_mem, tn=COL_TILE) for l in range(depth)]
    mem_k_p = jnp.stack([kv[:, :MEM_WIDTH] for kv in mem_kv])[:, None]
    mem_v_p = jnp.stack([kv[:, MEM_WIDTH:] for kv in mem_kv])[:, None]
    mem_k_s = cache_mem_k.reshape(depth, Bs, n_mem, MEM_WIDTH)
    mem_v_s = cache_mem_v.reshape(depth, Bs, n_mem, MEM_WIDTH)
    ffn1_w = (bf(ffn1_wg), bf(ffn1_wu), bf(ffn1_wd))
    ffn2_w = (bf(ffn2_wg), bf(ffn2_wu), bf(ffn2_wd))

    x = jnp.concatenate([x_prompt.reshape(Lp, D), x_sample.reshape(Ms, D)], axis=0)
    for layer in range(depth):
        x = dense_ffn(x, ffn1_norm[layer], *ffn1_w, layer)
        if layer % 2 == 0:
            proj = dense_in(x, mix_norm[layer], bf(ab_w_in))
            sproj = proj[Lp:].reshape(Bs, Ls, -1)
            q_s, k_s, v_s = [sproj[..., i * SB_WIDTH:(i + 1) * SB_WIDTH] for i in range(3)]
            a_p = sb_prompt(proj, sb_bias, L=Lp)
            a_s = sb_sample(q_s, k_s, v_s, cache_sb_k, cache_sb_v, page_table, sb_bias)
            b_p, lru_h_p, lru_c_p = lru_prompt(proj, 3, 4, jnp.zeros((CONV_W - 1, LRU_WIDTH), F32),
                                               jnp.zeros((1, LRU_WIDTH), F32), lru_p, L=Lp)
            b_s, lru_h_s, lru_c_s = lru_sample(tmajor(sproj[..., 3 * SB_WIDTH:3 * SB_WIDTH + LRU_WIDTH]),
                                               tmajor(sproj[..., 3 * SB_WIDTH + LRU_WIDTH:]),
                                               tmajor(state_lru_conv), state_lru_h, lru_p)
            x = dense_out(x, [(a_p, a_s.reshape(Ms, SB_WIDTH)), (b_p, tmajor(b_s).reshape(Ms, LRU_WIDTH))],
                          bf(ab_w_out))
            sb_k_p = proj[:Lp, SB_WIDTH:2 * SB_WIDTH].reshape(1, Lp, SB_HEADS, HEAD_DIM)
            sb_v_p = proj[:Lp, 2 * SB_WIDTH:3 * SB_WIDTH].reshape(1, Lp, SB_HEADS, HEAD_DIM)
            sb_k_s = k_s.reshape(Bs, Ls, SB_HEADS, HEAD_DIM)
            sb_v_s = v_s.reshape(Bs, Ls, SB_HEADS, HEAD_DIM)
            lru_c_p = lru_c_p[None]
            lru_c_s = tmajor(lru_c_s)
        else:
            proj = dense_in(x, mix_norm[layer], bf(cd_w_in))
            sproj = proj[Lp:].reshape(Bs, Ls, -1)
            pos_p, pos_s = jnp.arange(Lp), past_len + jnp.arange(Ls)
            c_p, ret_p = retention(proj[None], (0, 1, 2, 3), *rope_tables(pos_p, RET_DK // 2),
                                   jnp.zeros((1, RET_HEADS, RET_DK, RET_DK), F32), ret_norm, L=Lp)
            c_s, ret_s = retention(sproj, (0, 1, 2, 3), *rope_tables(pos_s, RET_DK // 2), state_ret, ret_norm, L=Ls)
            kv0 = 4 * RET_WIDTH + SWA_WIDTH
            d_p, k_rot_p = swa_prompt(proj, 4 * RET_WIDTH // SWA_WIDTH, kv0 // SWA_KV_WIDTH, kv0 // SWA_KV_WIDTH + 1,
                                      *rope_tables_full(pos_p), swa_sinks, L=Lp)
            d_s, k_rot_s = swa_sample(sproj[..., 4 * RET_WIDTH:kv0], sproj[..., kv0:kv0 + SWA_KV_WIDTH],
                                      sproj[..., kv0 + SWA_KV_WIDTH:], cache_swa_k.reshape(Bs, wb, SWA_KV_WIDTH),
                                      cache_swa_v.reshape(Bs, wb, SWA_KV_WIDTH), swa_sinks, past_len=past_len)
            x = dense_out(x, [(c_p[0], c_s.reshape(Ms, RET_WIDTH)), (d_p, d_s.reshape(Ms, SWA_WIDTH))],
                          bf(cd_w_out))
            wp = min(WINDOW, Lp)
            swa_k_p = k_rot_p[Lp - wp:].reshape(1, wp, SWA_KV_HEADS, HEAD_DIM)
            swa_v_p = proj[Lp - wp:Lp, kv0 + SWA_KV_WIDTH:].reshape(1, wp, SWA_KV_HEADS, HEAD_DIM)
            kv_heads = lambda t: t.reshape(Bs, Ls, SWA_KV_HEADS, HEAD_DIM)
            swa_k_s = jnp.concatenate([cache_swa_k, kv_heads(k_rot_s)], axis=1)[:, Ls:]
            swa_v_s = jnp.concatenate([cache_swa_v, kv_heads(sproj[..., kv0 + SWA_KV_WIDTH:])], axis=1)[:, Ls:]
        qx = dense_in(x, xattn_norm[layer], bf(xattn_wq[layer]))
        o_p = cross_attention(qx[None], mem_k_p, mem_v_p, layer, L=Lp, tq=ROW_TILE)
        o_s = cross_attention_packed(qx[Lp:].reshape(Bs, Ls, MEM_WIDTH), mem_k_s, mem_v_s, layer)
        x = dense_out(x, [(o_p[0], o_s.reshape(Ms, MEM_WIDTH))], bf(xattn_wo[layer]))
        x = dense_ffn(x, ffn2_norm[layer], *ffn2_w, layer)
    y = rmsnorm(x, final_norm, tm=ROW_TILE)
    mem_heads = lambda t: t.reshape(depth, 1, n_mem, MEM_HEADS, HEAD_DIM)
    return (y[:Lp].reshape(1, Lp, D), y[Lp:].reshape(Bs, Ls, D), sb_k_p, sb_v_p, sb_k_s, sb_v_s,
            lru_h_p, lru_h_s, lru_c_p, lru_c_s, ret_p, ret_s, swa_k_p, swa_v_p, swa_k_s, swa_v_s,
            mem_heads(mem_k_p), mem_heads(mem_v_p))
```

```python
import functools

import jax
import jax.numpy as jnp
import numpy as np
from jax import lax
from jax.experimental import pallas as pl
from jax.experimental.pallas import tpu as pltpu

F32 = jnp.float32
BF16 = jnp.bfloat16

HEAD_DIM = 128
SB_HEADS = 8
SB_WIDTH = SB_HEADS * HEAD_DIM
LRU_WIDTH = 1024
LRU_BLOCKS = 8
LRU_BLOCK = LRU_WIDTH // LRU_BLOCKS
CONV_W = 4
RG_C = 8.0
RET_HEADS = 4
RET_DK = 256
RET_WIDTH = RET_HEADS * RET_DK
RET_CHUNK = 128
SWA_HEADS = 8
SWA_KV_HEADS = 2
SWA_GROUP = SWA_HEADS // SWA_KV_HEADS
SWA_WIDTH = SWA_HEADS * HEAD_DIM
SWA_KV_WIDTH = SWA_KV_HEADS * HEAD_DIM
WINDOW = 128
MEM_HEADS = 4
MEM_WIDTH = MEM_HEADS * HEAD_DIM
PAGE_SIZE = 128
ROPE_THETA = 10000.0
EPS = 1e-6

VMEM_LIMIT_V7X = 56 * 1024 * 1024


def _params(*semantics):
    return pltpu.CompilerParams(dimension_semantics=semantics, vmem_limit_bytes=VMEM_LIMIT_V7X)


def _dot(a, b):
    return jnp.dot(a, b, preferred_element_type=F32)


def _dot_nt(a, b):
    return lax.dot_general(a, b, (((1,), (1,)), ((), ())), preferred_element_type=F32)


def _rms(x, g):
    return x * lax.rsqrt(jnp.mean(x * x, axis=-1, keepdims=True) + EPS) * g


def _softplus(x):
    return jnp.maximum(x, 0.0) + jnp.log1p(jnp.exp(-jnp.abs(x)))


def _expm1(x):
    u = jnp.exp(x)
    one = u == 1.0
    return jnp.where(one, x, (u - 1.0) * x / jnp.where(one, 1.0, jnp.log(u)))


def _norm_matmul_kernel(x_ref, g_ref, w_ref, o_ref, h_ref):
    @pl.when(pl.program_id(1) == 0)
    def _():
        h_ref[...] = _rms(x_ref[...], g_ref[...]).astype(BF16)

    o_ref[...] = _dot(h_ref[...], w_ref[...]).astype(o_ref.dtype)


def norm_matmul(x, g, w, *, tm, tn, out_dtype=F32):
    M, D = x.shape
    N = w.shape[1]
    return pl.pallas_call(
        _norm_matmul_kernel,
        out_shape=jax.ShapeDtypeStruct((M, N), out_dtype),
        grid=(M // tm, N // tn),
        in_specs=[pl.BlockSpec((tm, D), lambda i, j: (i, 0)),
                  pl.BlockSpec((1, D), lambda i, j: (0, 0)),
                  pl.BlockSpec((D, tn), lambda i, j: (0, j))],
        out_specs=pl.BlockSpec((tm, tn), lambda i, j: (i, j)),
        scratch_shapes=[pltpu.VMEM((tm, D), BF16)],
        compiler_params=_params("parallel", "arbitrary"),
        name="norm_matmul",
    )(x, g.reshape(1, D), w)


def _ffn_kernel(x_ref, g_ref, wg_ref, wu_ref, wd_ref, o_ref, h_ref, acc_ref):
    f = pl.program_id(1)

    @pl.when(f == 0)
    def _():
        h_ref[...] = _rms(x_ref[...], g_ref[...]).astype(BF16)
        acc_ref[...] = jnp.zeros_like(acc_ref)

    h = h_ref[...]
    a = _dot(h, wg_ref[...])
    b = _dot(h, wu_ref[...])
    t = (a * jax.nn.sigmoid(a) * b).astype(BF16)
    acc_ref[...] += _dot(t, wd_ref[...])

    @pl.when(f == pl.num_programs(1) - 1)
    def _():
        o_ref[...] = x_ref[...] + 0.5 * acc_ref[...]


def ffn(x, g, wg, wu, wd, layer, *, tm, tf):
    M, D = x.shape
    F = wg.shape[2]
    return pl.pallas_call(
        _ffn_kernel,
        out_shape=jax.ShapeDtypeStruct((M, D), F32),
        grid=(M // tm, F // tf),
        in_specs=[pl.BlockSpec((tm, D), lambda i, f: (i, 0)),
                  pl.BlockSpec((1, D), lambda i, f: (0, 0)),
                  pl.BlockSpec((None, D, tf), lambda i, f: (layer, 0, f)),
                  pl.BlockSpec((None, D, tf), lambda i, f: (layer, 0, f)),
                  pl.BlockSpec((None, tf, D), lambda i, f: (layer, f, 0))],
        out_specs=pl.BlockSpec((tm, D), lambda i, f: (i, 0)),
        scratch_shapes=[pltpu.VMEM((tm, D), BF16), pltpu.VMEM((tm, D), F32)],
        compiler_params=_params("parallel", "arbitrary"),
        name="ffn",
    )(x, g.reshape(1, D), wg, wu, wd)


def _matmul_residual_kernel(x_ref, *refs, n_parts, prompt_tiles):
    part_refs, (w_ref, o_ref) = refs[:2 * n_parts], refs[2 * n_parts:]

    def emit(group):
        acc = x_ref[...]
        row = 0
        for p in range(n_parts):
            a = part_refs[2 * p + group][...]
            acc = acc + _dot(a.astype(BF16), w_ref[row:row + a.shape[1], :])
            row += a.shape[1]
        o_ref[...] = acc

    i = pl.program_id(0)
    pl.when(i < prompt_tiles)(lambda: emit(0))
    pl.when(i >= prompt_tiles)(lambda: emit(1))


def matmul_residual(x, parts, w, *, tm, tn):
    M, N = x.shape
    Lp = parts[0][0].shape[0]
    assert Lp % tm == 0 and (M - Lp) % tm == 0 and all(s.shape[0] == M - Lp for _, s in parts)
    prompt_tiles = Lp // tm
    specs, args = [], []
    for prompt, sample in parts:
        k = prompt.shape[1]
        specs += [pl.BlockSpec((tm, k), lambda i, j: (jnp.minimum(i, prompt_tiles - 1), 0)),
                  pl.BlockSpec((tm, k), lambda i, j: (jnp.maximum(i - prompt_tiles, 0), 0))]
        args += [prompt, sample]
    return pl.pallas_call(
        functools.partial(_matmul_residual_kernel, n_parts=len(parts), prompt_tiles=prompt_tiles),
        out_shape=jax.ShapeDtypeStruct((M, N), F32),
        grid=(M // tm, N // tn),
        in_specs=[pl.BlockSpec((tm, tn), lambda i, j: (i, j))] + specs
                 + [pl.BlockSpec((w.shape[0], tn), lambda i, j: (0, j))],
        out_specs=pl.BlockSpec((tm, tn), lambda i, j: (i, j)),
        compiler_params=_params("parallel", "parallel"),
        name="matmul_residual",
    )(x, *args, w)


def _rmsnorm_kernel(x_ref, g_ref, o_ref):
    o_ref[...] = _rms(x_ref[...], g_ref[...])


def rmsnorm(x, g, *, tm):
    M, D = x.shape
    return pl.pallas_call(
        _rmsnorm_kernel,
        out_shape=jax.ShapeDtypeStruct((M, D), F32),
        grid=(M // tm,),
        in_specs=[pl.BlockSpec((tm, D), lambda i: (i, 0)), pl.BlockSpec((1, D), lambda i: (0, 0))],
        out_specs=pl.BlockSpec((tm, D), lambda i: (i, 0)),
        compiler_params=_params("parallel"),
        name="rmsnorm",
    )(x, g.reshape(1, D))


LOG2E = float(np.log2(np.e))
SB_SCALE2 = HEAD_DIM ** -0.5 * LOG2E


def _split_dot(tri_first, x, tri2):
    hi = x.astype(BF16)
    lo = (x - hi.astype(F32)).astype(BF16)
    if tri_first:
        return _dot(tri2, jnp.concatenate([hi, lo], axis=0))
    return _dot(jnp.concatenate([hi, lo], axis=-1), tri2)


def _sb_scores(logits, mask):
    sp = jnp.log2(1.0 + jnp.exp2(-jnp.abs(logits)))
    log_beta = jnp.minimum(logits, 0.0) - sp
    log_rest = log_beta - logits
    if mask is not None:
        log_rest = jnp.where(mask, log_rest, 0.0)
    return log_beta, log_rest


def _sb_prompt_tiles(q, k, v, bias, tri, carry, mask, n_sub):
    blk = tri.shape[1]
    log_beta, log_rest = _sb_scores(_dot_nt(q, k) * SB_SCALE2 + bias, mask)
    ws = [None] * n_sub
    for s in reversed(range(n_sub)):
        cols = slice(s * blk, (s + 1) * blk)
        after = carry + _split_dot(False, log_rest[:, cols], tri)
        w = jnp.exp2(log_beta[:, cols] + after)
        ws[s] = (w if mask is None else jnp.where(mask, w, 0.0)).astype(BF16)
        carry = carry + jnp.sum(log_rest[:, cols], axis=-1, keepdims=True)
    w_all = ws[0] if n_sub == 1 else jnp.concatenate(ws, axis=-1)
    return _dot(w_all, v), carry


SB_SWEEP = 4


def _sb_prompt_kernel(bias_ref, q_ref, k_ref, v_ref, o_ref, kb_ref, vb_ref, *, blk):
    h = pl.program_id(0)
    i = pl.program_id(1)
    L = k_ref.shape[0]

    @pl.when(i == 0)
    def _():
        def cast(n, c):
            s = pl.multiple_of(n * blk, blk)
            kb_ref[pl.ds(s, blk), :] = k_ref[pl.ds(s, blk), :].astype(BF16)
            vb_ref[pl.ds(s, blk), :] = v_ref[pl.ds(s, blk), :].astype(BF16)
            return c
        lax.fori_loop(0, L // blk, cast, 0)

    q = q_ref[...].astype(BF16)
    bias = bias_ref[h] * LOG2E
    row = lax.broadcasted_iota(jnp.int32, (blk, blk), 0)
    col = lax.broadcasted_iota(jnp.int32, (blk, blk), 1)
    tri = jnp.where(row > col, 1.0, 0.0).astype(BF16)
    tri = jnp.concatenate([tri, tri], axis=0)

    def sweep(first_block, n_sub, mask, state):
        acc, carry = state
        s = pl.multiple_of(first_block * blk, blk)
        a, carry = _sb_prompt_tiles(q, kb_ref[pl.ds(s, n_sub * blk), :], vb_ref[pl.ds(s, n_sub * blk), :], bias, tri,
                                    carry, mask, n_sub)
        return acc + a, carry

    state = sweep(i, 1, col < row, (jnp.zeros((blk, HEAD_DIM), F32), jnp.zeros((blk, 1), F32)))
    odd = i % SB_SWEEP
    state = lax.fori_loop(0, odd, lambda n, st: sweep(i - 1 - n, 1, None, st), state)
    state = lax.fori_loop(0, i // SB_SWEEP, lambda n, st: sweep(i - odd - SB_SWEEP * (n + 1), SB_SWEEP, None, st),
                          state)
    o_ref[...] = state[0].astype(o_ref.dtype)


def sb_prompt(qkv, bias, *, L, blk=256):
    blk = min(blk, L)
    assert L % blk == 0
    kern = functools.partial(_sb_prompt_kernel, blk=blk)
    return pl.pallas_call(
        kern,
        out_shape=jax.ShapeDtypeStruct((L, SB_WIDTH), BF16),
        grid=(SB_HEADS, L // blk),
        in_specs=[pl.BlockSpec(memory_space=pltpu.SMEM),
                  pl.BlockSpec((blk, HEAD_DIM), lambda h, i: (i, h)),
                  pl.BlockSpec((L, HEAD_DIM), lambda h, i: (0, SB_HEADS + h)),
                  pl.BlockSpec((L, HEAD_DIM), lambda h, i: (0, 2 * SB_HEADS + h))],
        out_specs=pl.BlockSpec((blk, HEAD_DIM), lambda h, i: (i, h)),
        scratch_shapes=[pltpu.VMEM((L, HEAD_DIM), BF16), pltpu.VMEM((L, HEAD_DIM), BF16)],
        compiler_params=_params("arbitrary", "arbitrary"),
        name="sb_prompt",
    )(bias, qkv, qkv, qkv)


SB_LANES = 128
SB_PACK = 4
SB_STEP = 16


def _page_rows(ref):
    keys = ref.shape[1] // SB_HEADS
    return jnp.concatenate([ref[0, pl.ds(h, keys, stride=SB_HEADS), :] for h in range(SB_HEADS)], axis=-1)


def _fold_lanes(t, width):
    out = t
    for j in range(1, SB_LANES // width):
        out = out + pltpu.roll(t, j * width, 1)
    return out


def _sb_sample_tile(ks, vs, qbd_ref, bias_row, tri_t, carry, mask, width):
    logits = None
    for i, k in enumerate(ks):
        part = _dot(k.astype(BF16), qbd_ref[i])
        logits = part if logits is None else logits + part
    log_beta, log_rest = _sb_scores(logits * SB_SCALE2 + bias_row, mask)
    tot = jnp.sum(log_rest, axis=0, keepdims=True)
    lane = lax.broadcasted_iota(jnp.int32, tot.shape, 1)
    within = jnp.zeros_like(tot)
    for j in range(1, len(ks)):
        within = within + jnp.where(lane >= j * width, pltpu.roll(tot, j * width, 1), 0.0)
    after = carry + within + _split_dot(True, log_rest, tri_t)
    w = jnp.exp2(log_beta + after)
    if mask is not None:
        w = jnp.where(mask, w, 0.0)
    wt = jnp.transpose(w).astype(BF16)
    out = None
    for i, v in enumerate(vs):
        part = _dot(wt[i * width:(i + 1) * width], v.astype(BF16))
        out = part if out is None else out + part
    return out, carry + _fold_lanes(tot, width)


def _sb_sample_kernel(pt_ref, qbd_ref, bias_ref, knew_ref, vnew_ref, *refs, n_q):
    kp, vp = refs[:SB_STEP], refs[SB_STEP:2 * SB_STEP]
    o_ref, acc_ref, carry_ref, qbd4_ref = refs[2 * SB_STEP:]
    s = pl.program_id(1)
    keys = knew_ref.shape[1]
    width = SB_HEADS * n_q
    row = lax.broadcasted_iota(jnp.int32, (keys, SB_LANES), 0)
    col = lax.broadcasted_iota(jnp.int32, (keys, SB_LANES), 1)
    tri_t = jnp.where(col > row, 1.0, 0.0).astype(BF16)
    tri_t = jnp.concatenate([tri_t, tri_t], axis=1)
    bias_row = bias_ref[...] * LOG2E

    @pl.when(s == 0)
    def _():
        q = qbd_ref[0]
        for i in range(SB_PACK):
            qbd4_ref[i] = (q if i == 0 else pltpu.roll(q, i * width, 1)).astype(BF16)
        mask = (row < col % n_q) & (col < width)
        a, c = _sb_sample_tile([knew_ref[0]], [vnew_ref[0]], qbd4_ref, bias_row, tri_t,
                               jnp.zeros((1, SB_LANES), F32), mask, width)
        acc_ref[...] = a
        carry_ref[...] = c

    carry = carry_ref[...]
    acc = acc_ref[...]
    for g in range(SB_STEP // SB_PACK):
        sel = range(g * SB_PACK, (g + 1) * SB_PACK)
        a, carry = _sb_sample_tile([_page_rows(kp[j]) for j in sel], [_page_rows(vp[j]) for j in sel], qbd4_ref,
                                   bias_row, tri_t, carry, None, width)
        acc = acc + a
    acc_ref[...] = acc
    carry_ref[...] = carry

    @pl.when(s == pl.num_programs(1) - 1)
    def _():
        for h in range(SB_HEADS):
            o_ref[0, :, h * HEAD_DIM:(h + 1) * HEAD_DIM] = (
                acc[h * n_q:(h + 1) * n_q, h * HEAD_DIM:(h + 1) * HEAD_DIM].astype(o_ref.dtype))


def sb_sample(q, k, v, pool_k, pool_v, page_table, bias):
    B, n_q, _ = q.shape
    n_pool = pool_k.shape[0]
    n_pages = page_table.shape[1]
    width = SB_HEADS * n_q
    assert width * SB_PACK == SB_LANES and n_q <= PAGE_SIZE and n_pages % SB_STEP == 0 and SB_STEP % SB_PACK == 0
    eye = jnp.eye(SB_HEADS, dtype=F32)
    qh = q.reshape(B, n_q, SB_HEADS, HEAD_DIM)
    qbd = jnp.einsum('bthd,hg->bhdgt', qh, eye).reshape(B, SB_WIDTH, width)
    qbd = jnp.pad(qbd, ((0, 0), (0, 0), (0, SB_LANES - width)))
    bias_row = jnp.tile(jnp.repeat(bias.astype(F32), n_q), SB_PACK).reshape(1, SB_LANES)
    knew = jnp.pad(k, ((0, 0), (0, PAGE_SIZE - n_q), (0, 0)))
    vnew = jnp.pad(v, ((0, 0), (0, PAGE_SIZE - n_q), (0, 0)))
    pool_k = pool_k.reshape(n_pool, PAGE_SIZE * SB_HEADS, HEAD_DIM)
    pool_v = pool_v.reshape(n_pool, PAGE_SIZE * SB_HEADS, HEAD_DIM)

    def page_spec(j):
        return pl.BlockSpec((1, PAGE_SIZE * SB_HEADS, HEAD_DIM),
                            lambda b, s, pt: (pt[b * n_pages + n_pages - 1 - s * SB_STEP - j], 0, 0))

    per_seq = lambda b, s, pt: (b, 0, 0)
    grid_spec = pltpu.PrefetchScalarGridSpec(
        num_scalar_prefetch=1,
        grid=(B, n_pages // SB_STEP),
        in_specs=[pl.BlockSpec((1, SB_WIDTH, SB_LANES), per_seq),
                  pl.BlockSpec((1, SB_LANES), lambda b, s, pt: (0, 0)),
                  pl.BlockSpec((1, PAGE_SIZE, SB_WIDTH), per_seq),
                  pl.BlockSpec((1, PAGE_SIZE, SB_WIDTH), per_seq)]
                 + [page_spec(j) for j in range(SB_STEP)] * 2,
        out_specs=pl.BlockSpec((1, n_q, SB_WIDTH), per_seq),
        scratch_shapes=[pltpu.VMEM((width, SB_WIDTH), F32), pltpu.VMEM((1, SB_LANES), F32),
                        pltpu.VMEM((SB_PACK, SB_WIDTH, SB_LANES), BF16)],
    )
    return pl.pallas_call(
        functools.partial(_sb_sample_kernel, n_q=n_q),
        out_shape=jax.ShapeDtypeStruct((B, n_q, SB_WIDTH), BF16),
        grid_spec=grid_spec,
        compiler_params=_params("parallel", "arbitrary"),
        name="sb_sample",
    )(page_table.reshape(-1), qbd, bias_row, knew, vnew, *([pool_k] * SB_STEP), *([pool_v] * SB_STEP))


def _lru_coeffs(y, wa_ref, wx_ref, ba, bx, sp_lam):
    yb = y.astype(BF16)
    r_parts, i_parts = [], []
    for n in range(LRU_BLOCKS):
        yn = yb[:, n * LRU_BLOCK:(n + 1) * LRU_BLOCK]
        r_parts.append(_dot(yn, wa_ref[n]))
        i_parts.append(_dot(yn, wx_ref[n]))
    r = jax.nn.sigmoid(jnp.concatenate(r_parts, axis=-1) + ba)
    i = jax.nn.sigmoid(jnp.concatenate(i_parts, axis=-1) + bx)
    log_a = -RG_C * r * sp_lam
    return jnp.exp(log_a), jnp.sqrt(-_expm1(2.0 * log_a)) * (i * y)


LRU_PAD = 8


def _lru_prompt_kernel(gate_ref, xb_ref, conv0_ref, h0_ref, cw_ref, cb_ref, wa_ref, wx_ref, ba_ref, bx_ref,
                       lam_ref, o_ref, hlast_ref, cnew_ref, ext_ref, h_ref, a_ref, b_ref, hs_ref, *, T):
    i = pl.program_id(0)
    tail = CONV_W - 1

    @pl.when(i == 0)
    def _():
        ext_ref[LRU_PAD - tail:LRU_PAD, :] = conv0_ref[...]
        h_ref[...] = h0_ref[...]

    ext_ref[LRU_PAD:LRU_PAD + T, :] = xb_ref[...]
    y = cb_ref[...]
    for j in range(CONV_W):
        y = y + cw_ref[j:j + 1, :] * ext_ref[LRU_PAD - tail + j:LRU_PAD - tail + j + T, :]
    a, b = _lru_coeffs(y, wa_ref, wx_ref, ba_ref[...], bx_ref[...], _softplus(-lam_ref[...]))
    a_ref[...] = a
    b_ref[...] = b

    def step(t, h):
        h = a_ref[pl.ds(t, 1), :] * h + b_ref[pl.ds(t, 1), :]
        hs_ref[pl.ds(t, 1), :] = h
        return h

    h = lax.fori_loop(0, T, step, h_ref[...], unroll=8)
    h_ref[...] = h
    o_ref[...] = (jax.nn.gelu(gate_ref[...]) * hs_ref[...]).astype(o_ref.dtype)
    last_rows = ext_ref[LRU_PAD + T - tail:LRU_PAD + T, :]
    ext_ref[LRU_PAD - tail:LRU_PAD, :] = last_rows

    @pl.when(i == pl.num_programs(0) - 1)
    def _():
        hlast_ref[...] = h
        cnew_ref[...] = last_rows


def _lru_weights(p):
    row = lambda t: t.reshape(1, LRU_WIDTH)
    return (p['lru_conv_w'], row(p['lru_conv_b']), p['lru_wa'].astype(BF16), p['lru_wx'].astype(BF16),
            row(p['lru_ba']), row(p['lru_bx']), row(p['lru_lambda']))


def _lru_weight_specs():
    zero2 = lambda i: (0, 0)
    zero3 = lambda i: (0, 0, 0)
    vec = pl.BlockSpec((1, LRU_WIDTH), zero2)
    mat = pl.BlockSpec((LRU_BLOCKS, LRU_BLOCK, LRU_BLOCK), zero3)
    return [pl.BlockSpec((CONV_W, LRU_WIDTH), zero2), vec, mat, mat, vec, vec, vec]


def lru_prompt(proj, gate_col, xb_col, conv0, h0, p, *, L, T=256):
    T = min(T, L)
    assert L % T == 0 and T >= CONV_W - 1
    W = LRU_WIDTH
    fixed = lambda i: (0, 0)
    return pl.pallas_call(
        functools.partial(_lru_prompt_kernel, T=T),
        out_shape=(jax.ShapeDtypeStruct((L, W), BF16), jax.ShapeDtypeStruct((1, W), F32),
                   jax.ShapeDtypeStruct((CONV_W - 1, W), F32)),
        grid=(L // T,),
        in_specs=[pl.BlockSpec((T, W), lambda i: (i, gate_col)), pl.BlockSpec((T, W), lambda i: (i, xb_col)),
                  pl.BlockSpec((CONV_W - 1, W), fixed), pl.BlockSpec((1, W), fixed)] + _lru_weight_specs(),
        out_specs=(pl.BlockSpec((T, W), lambda i: (i, 0)), pl.BlockSpec((1, W), fixed),
                   pl.BlockSpec((CONV_W - 1, W), fixed)),
        scratch_shapes=[pltpu.VMEM((LRU_PAD + T, W), F32), pltpu.VMEM((1, W), F32), pltpu.VMEM((T, W), F32),
                        pltpu.VMEM((T, W), F32), pltpu.VMEM((T, W), F32)],
        compiler_params=_params("arbitrary"),
        name="lru_prompt",
    )(proj, proj, conv0, h0, *_lru_weights(p))


def _lru_sample_kernel(gate_ref, xb_ref, conv0_ref, h0_ref, cw_ref, cb_ref, wa_ref, wx_ref, ba_ref, bx_ref,
                       lam_ref, o_ref, hlast_ref, cnew_ref, *, L):
    tail = CONV_W - 1
    xs = [conv0_ref[j] for j in range(tail)] + [xb_ref[t] for t in range(L)]
    sp_lam = _softplus(-lam_ref[...])
    h = h0_ref[...]
    for t in range(L):
        y = cb_ref[...]
        for j in range(CONV_W):
            y = y + cw_ref[j:j + 1, :] * xs[t + j]
        a, b = _lru_coeffs(y, wa_ref, wx_ref, ba_ref[...], bx_ref[...], sp_lam)
        h = a * h + b
        o_ref[t] = (jax.nn.gelu(gate_ref[t]) * h).astype(o_ref.dtype)
    hlast_ref[...] = h
    for j in range(tail):
        cnew_ref[j] = xs[L + j]


def lru_sample(gate_t, xb_t, conv0_t, h0, p, *, tb=64):
    L, B, W = xb_t.shape
    tb = min(tb, B)
    assert B % tb == 0
    tmaj = lambda n: pl.BlockSpec((n, tb, W), lambda i: (0, i, 0))
    rows = pl.BlockSpec((tb, W), lambda i: (i, 0))
    return pl.pallas_call(
        functools.partial(_lru_sample_kernel, L=L),
        out_shape=(jax.ShapeDtypeStruct((L, B, W), BF16), jax.ShapeDtypeStruct((B, W), F32),
                   jax.ShapeDtypeStruct((CONV_W - 1, B, W), F32)),
        grid=(B // tb,),
        in_specs=[tmaj(L), tmaj(L), tmaj(CONV_W - 1), rows] + _lru_weight_specs(),
        out_specs=(tmaj(L), rows, tmaj(CONV_W - 1)),
        compiler_params=_params("parallel"),
        name="lru_sample",
    )(gate_t, xb_t, conv0_t, h0, *_lru_weights(p))


def rope_tables(pos, half):
    inv = ROPE_THETA ** (-jnp.arange(half, dtype=F32) / half)
    ang = pos.astype(F32)[:, None] * inv[None, :]
    return jnp.cos(ang), jnp.sin(ang)


def _dot_tn(a, b):
    return lax.dot_general(a, b, (((0,), (0,)), ((), ())), preferred_element_type=F32)


def _rope_halves(x, cos, sin):
    half = x.shape[-1] // 2
    x1, x2 = x[:, :half], x[:, half:]
    return jnp.concatenate([x1 * cos - x2 * sin, x1 * sin + x2 * cos], axis=-1)


def _retention_kernel(sdec_ref, q_ref, k_ref, v_ref, g_ref, cos_ref, sin_ref, intra_ref, qd_ref, kd_ref, norm_ref,
                      s0_ref, o_ref, sout_ref, s_ref):
    n = pl.program_id(1)

    @pl.when(n == 0)
    def _():
        s_ref[...] = s0_ref[0]

    cos, sin = cos_ref[...], sin_ref[...]
    for h in range(RET_HEADS):
        cols = slice(h * RET_DK, (h + 1) * RET_DK)
        q = _rope_halves(q_ref[0, :, cols], cos, sin)
        k = _rope_halves(k_ref[0, :, cols], cos, sin) * (RET_DK ** -0.5)
        v = v_ref[0, :, cols].astype(BF16)
        qb = q.astype(BF16)
        s = s_ref[h]
        att = _dot_nt(qb, k.astype(BF16)) * intra_ref[h]
        o = _dot(att.astype(BF16), v) + _dot(qb, s.astype(BF16)) * qd_ref[h]
        s_ref[h] = sdec_ref[h] * s + _dot_tn((k * kd_ref[h]).astype(BF16), v)
        o = o * lax.rsqrt(jnp.mean(o * o, axis=-1, keepdims=True) + EPS) * norm_ref[h:h + 1, :]
        g = g_ref[0, :, cols]
        o_ref[0, :, cols] = (o * (g * jax.nn.sigmoid(g))).astype(o_ref.dtype)

    @pl.when(n == pl.num_programs(1) - 1)
    def _():
        sout_ref[0] = s_ref[...]


def retention(proj, cols, cos, sin, s0, ret_norm, *, L):
    B = proj.shape[0]
    c = RET_CHUNK if L % RET_CHUNK == 0 else L
    H = RET_HEADS
    log_g = jnp.log1p(-jnp.exp2(-5.0 - jnp.arange(H, dtype=F32)))
    idx = jnp.arange(c, dtype=F32)
    diff = idx[:, None] - idx[None, :]
    causal = diff >= 0
    intra = jnp.where(causal[None], jnp.exp(jnp.where(causal, diff, 0.0)[None] * log_g[:, None, None]), 0.0)
    q_decay = jnp.exp((idx[None, :] + 1.0) * log_g[:, None])[:, :, None]
    k_decay = jnp.exp((c - 1.0 - idx)[None, :] * log_g[:, None])[:, :, None]
    s_decay = jnp.exp(c * log_g)
    blk = lambda col: pl.BlockSpec((1, c, RET_WIDTH), lambda b, n, col=col: (b, n, col))
    table = pl.BlockSpec((c, RET_DK // 2), lambda b, n: (n, 0))
    fixed3 = lambda shape: pl.BlockSpec(shape, lambda b, n: (0, 0, 0))
    state = pl.BlockSpec((1, H, RET_DK, RET_DK), lambda b, n: (b, 0, 0, 0))
    return pl.pallas_call(
        _retention_kernel,
        out_shape=(jax.ShapeDtypeStruct((B, L, RET_WIDTH), BF16), jax.ShapeDtypeStruct(s0.shape, F32)),
        grid=(B, L // c),
        in_specs=[pl.BlockSpec(memory_space=pltpu.SMEM)] + [blk(col) for col in cols]
                 + [table, table, fixed3((H, c, c)), fixed3((H, c, 1)), fixed3((H, c, 1)),
                    pl.BlockSpec((H, RET_DK), lambda b, n: (0, 0)), state],
        out_specs=(pl.BlockSpec((1, c, RET_WIDTH), lambda b, n: (b, n, 0)), state),
        scratch_shapes=[pltpu.VMEM((H, RET_DK, RET_DK), F32)],
        compiler_params=_params("parallel", "arbitrary"),
        name="retention",
    )(s_decay, proj, proj, proj, proj, cos, sin, intra, q_decay, k_decay, ret_norm, s0)


ATTN_SCALE = HEAD_DIM ** -0.5


def rope_tables_full(pos):
    cos, sin = rope_tables(pos, HEAD_DIM // 2)
    return jnp.concatenate([cos, cos], axis=-1), jnp.concatenate([-sin, sin], axis=-1)


def _rope_roll(x, cos_full, sin_signed):
    return x * cos_full + pltpu.roll(x, HEAD_DIM // 2, 1) * sin_signed


def _sink_softmax(s, mask, sink):
    s = jnp.where(mask, s, -jnp.inf)
    m = jnp.maximum(jnp.max(s, axis=-1, keepdims=True), sink)
    e = jnp.exp(s - m)
    return e / (jnp.sum(e, axis=-1, keepdims=True) + jnp.exp(sink - m))


def _swa_prompt_kernel(sinks_ref, q_ref, kc_ref, kp_ref, vc_ref, vp_ref, cc_ref, sc_ref, cp_ref, sp_ref,
                       o_ref, krot_ref, *, blk):
    j = pl.program_id(0)
    r = lax.broadcasted_iota(jnp.int32, (blk, 2 * blk), 0)
    c = lax.broadcasted_iota(jnp.int32, (blk, 2 * blk), 1)
    diff = blk + r - c
    mask = ((j - 1) * blk + c >= 0) & (diff >= 0) & (diff < WINDOW)
    cos_c, sin_c, cos_p, sin_p = cc_ref[...], sc_ref[...], cp_ref[...], sp_ref[...]
    for g in range(SWA_KV_HEADS):
        kv = slice(g * HEAD_DIM, (g + 1) * HEAD_DIM)
        k_cur = _rope_roll(kc_ref[:, kv], cos_c, sin_c)
        krot_ref[:, kv] = k_cur
        k_all = jnp.concatenate([_rope_roll(kp_ref[:, kv], cos_p, sin_p), k_cur], axis=0).astype(BF16)
        v_all = jnp.concatenate([vp_ref[:, kv], vc_ref[:, kv]], axis=0).astype(BF16)
        for hh in range(SWA_GROUP):
            h = g * SWA_GROUP + hh
            cols = slice(h * HEAD_DIM, (h + 1) * HEAD_DIM)
            q = _rope_roll(q_ref[:, cols], cos_c, sin_c).astype(BF16)
            p = _sink_softmax(_dot_nt(q, k_all) * ATTN_SCALE, mask, sinks_ref[h])
            o_ref[:, cols] = _dot(p.astype(BF16), v_all).astype(o_ref.dtype)


def swa_prompt(proj, q_col, k_col, v_col, cos_full, sin_signed, sinks, *, L, blk=128):
    assert L % blk == 0 and blk == WINDOW
    cur = lambda j: (j, 0)
    prev = lambda j: (jnp.maximum(j - 1, 0), 0)
    kv = lambda col, prv: pl.BlockSpec((blk, SWA_KV_WIDTH),
                                       (lambda j: (jnp.maximum(j - 1, 0), col)) if prv else (lambda j: (j, col)))
    table = lambda m: pl.BlockSpec((blk, HEAD_DIM), m)
    return pl.pallas_call(
        functools.partial(_swa_prompt_kernel, blk=blk),
        out_shape=(jax.ShapeDtypeStruct((L, SWA_WIDTH), BF16), jax.ShapeDtypeStruct((L, SWA_KV_WIDTH), F32)),
        grid=(L // blk,),
        in_specs=[pl.BlockSpec(memory_space=pltpu.SMEM),
                  pl.BlockSpec((blk, SWA_WIDTH), lambda j: (j, q_col)),
                  kv(k_col, False), kv(k_col, True), kv(v_col, False), kv(v_col, True),
                  table(cur), table(cur), table(prev), table(prev)],
        out_specs=(pl.BlockSpec((blk, SWA_WIDTH), cur), pl.BlockSpec((blk, SWA_KV_WIDTH), cur)),
        compiler_params=_params("parallel"),
        name="swa_prompt",
    )(sinks, proj, proj, proj, proj, proj, cos_full, sin_signed, cos_full, sin_signed)


def _swa_sample_kernel(q_ref, knew_ref, vnew_ref, bk_ref, bv_ref, cq_ref, sq_ref, ck_ref, sk_ref, sink_ref,
                       o_ref, knr_ref, kall_ref, vall_ref, *, n_new, past_len):
    wb = bk_ref.shape[1]
    rows = q_ref.shape[2]
    r = lax.broadcasted_iota(jnp.int32, (rows, 2 * wb), 0)
    c = lax.broadcasted_iota(jnp.int32, (rows, 2 * wb), 1)
    diff = r // SWA_GROUP + wb - c
    mask = (past_len - wb + c >= 0) & (diff >= 0) & (diff < WINDOW) & (c < wb + n_new)
    for b, g in [(b, g) for b in range(q_ref.shape[0]) for g in range(SWA_KV_HEADS)]:
        kv = slice(g * HEAD_DIM, (g + 1) * HEAD_DIM)
        k_new = _rope_roll(knew_ref[b, :, kv], ck_ref[...], sk_ref[...])
        knr_ref[b, :, kv] = k_new
        kall_ref[0:wb, :] = bk_ref[b, :, kv]
        kall_ref[wb:2 * wb, :] = jnp.zeros((wb, HEAD_DIM), F32)
        kall_ref[wb:wb + n_new, :] = k_new
        vall_ref[0:wb, :] = bv_ref[b, :, kv]
        vall_ref[wb:2 * wb, :] = jnp.zeros((wb, HEAD_DIM), F32)
        vall_ref[wb:wb + n_new, :] = vnew_ref[b, :, kv]
        q = _rope_roll(q_ref[b, g], cq_ref[...], sq_ref[...]).astype(BF16)
        p = _sink_softmax(_dot_nt(q, kall_ref[...].astype(BF16)) * ATTN_SCALE, mask, sink_ref[g])
        o_ref[b, g] = _dot(p.astype(BF16), vall_ref[...].astype(BF16)).astype(o_ref.dtype)


def swa_sample(qs, ks, vs, buf_k, buf_v, sinks, *, past_len, tb=8):
    B, n, _ = qs.shape
    wb = buf_k.shape[1]
    assert n <= wb
    rows = n * SWA_GROUP
    pos = past_len + jnp.arange(n)
    cos_k, sin_k = rope_tables_full(pos)
    cos_q, sin_q = jnp.repeat(cos_k, SWA_GROUP, axis=0), jnp.repeat(sin_k, SWA_GROUP, axis=0)
    q_g = qs.reshape(B, n, SWA_KV_HEADS, SWA_GROUP, HEAD_DIM).transpose(0, 2, 1, 3, 4).reshape(
        B, SWA_KV_HEADS, rows, HEAD_DIM)
    sink_g = jnp.tile(sinks.astype(F32).reshape(SWA_KV_HEADS, 1, SWA_GROUP), (1, n, 1)).reshape(
        SWA_KV_HEADS, rows, 1)
    tb = min(tb, B)
    assert B % tb == 0
    per_b3 = lambda shape: pl.BlockSpec((tb,) + shape, lambda b: (b, 0, 0))
    per_b4 = pl.BlockSpec((tb, SWA_KV_HEADS, rows, HEAD_DIM), lambda b: (b, 0, 0, 0))
    fixed2 = lambda shape: pl.BlockSpec(shape, lambda b: (0, 0))
    o, k_rot = pl.pallas_call(
        functools.partial(_swa_sample_kernel, n_new=n, past_len=past_len),
        out_shape=(jax.ShapeDtypeStruct((B, SWA_KV_HEADS, rows, HEAD_DIM), BF16),
                   jax.ShapeDtypeStruct((B, n, SWA_KV_WIDTH), F32)),
        grid=(B // tb,),
        in_specs=[per_b4, per_b3((n, SWA_KV_WIDTH)), per_b3((n, SWA_KV_WIDTH)),
                  per_b3((wb, SWA_KV_WIDTH)), per_b3((wb, SWA_KV_WIDTH)),
                  fixed2((rows, HEAD_DIM)), fixed2((rows, HEAD_DIM)), fixed2((n, HEAD_DIM)), fixed2((n, HEAD_DIM)),
                  pl.BlockSpec((SWA_KV_HEADS, rows, 1), lambda b: (0, 0, 0))],
        out_specs=(per_b4, per_b3((n, SWA_KV_WIDTH))),
        scratch_shapes=[pltpu.VMEM((2 * wb, HEAD_DIM), F32), pltpu.VMEM((2 * wb, HEAD_DIM), F32)],
        compiler_params=_params("parallel"),
        name="swa_sample",
    )(q_g, ks, vs, buf_k, buf_v, cos_q, sin_q, cos_k, sin_k, sink_g)
    d_out = o.reshape(B, SWA_KV_HEADS, n, SWA_GROUP, HEAD_DIM).transpose(0, 2, 1, 3, 4).reshape(B, n, SWA_WIDTH)
    return d_out, k_rot


def _xattn_kernel(q_ref, k_ref, v_ref, o_ref):
    for b in range(q_ref.shape[0]):
        for h in range(MEM_HEADS):
            cols = slice(h * HEAD_DIM, (h + 1) * HEAD_DIM)
            s = _dot_nt(q_ref[b, :, cols].astype(BF16), k_ref[b, :, cols].astype(BF16)) * ATTN_SCALE
            e = jnp.exp(s - jnp.max(s, axis=-1, keepdims=True))
            p = e / jnp.sum(e, axis=-1, keepdims=True)
            o_ref[b, :, cols] = _dot(p.astype(BF16), v_ref[b, :, cols].astype(BF16)).astype(o_ref.dtype)


def cross_attention(q, mem_k, mem_v, layer, *, L, tq):
    B = q.shape[0]
    n_mem = mem_k.shape[2]
    mem = pl.BlockSpec((None, 1, n_mem, MEM_WIDTH), lambda b, i: (layer, b, 0, 0))
    rows = pl.BlockSpec((1, tq, MEM_WIDTH), lambda b, i: (b, i, 0))
    return pl.pallas_call(
        _xattn_kernel,
        out_shape=jax.ShapeDtypeStruct((B, L, MEM_WIDTH), BF16),
        grid=(B, L // tq),
        in_specs=[rows, mem, mem],
        out_specs=rows,
        compiler_params=_params("parallel", "parallel"),
        name="cross_attention",
    )(q, mem_k, mem_v)


XATTN_LANES = 128


def _mem_rows(ref, j):
    slots = ref.shape[1] // MEM_HEADS
    return jnp.concatenate([ref[j, pl.ds(h, slots, stride=MEM_HEADS), :] for h in range(MEM_HEADS)], axis=-1)


def _xattn_packed_kernel(qbd_ref, k_ref, v_ref, o_ref, *, n_q):
    n_seq = k_ref.shape[0]
    width = MEM_HEADS * n_q
    s = None
    for j in range(n_seq):
        part = _dot(_mem_rows(k_ref, j).astype(BF16), qbd_ref[j])
        s = part if s is None else s + part
    s = s * ATTN_SCALE
    e = jnp.exp(s - jnp.max(s, axis=0, keepdims=True))
    p = e / jnp.sum(e, axis=0, keepdims=True)
    pt = jnp.transpose(p).astype(BF16)
    for j in range(n_seq):
        o = _dot(pt[j * width:(j + 1) * width], _mem_rows(v_ref, j).astype(BF16))
        for h in range(MEM_HEADS):
            cols = slice(h * HEAD_DIM, (h + 1) * HEAD_DIM)
            o_ref[j, :, cols] = o[h * n_q:(h + 1) * n_q, cols].astype(o_ref.dtype)


def cross_attention_packed(q, mem_k, mem_v, layer):
    B, n_q, _ = q.shape
    mem_rows = mem_k.shape[2]
    width = MEM_HEADS * n_q
    n_seq = XATTN_LANES // width
    assert n_seq * width == XATTN_LANES and B % n_seq == 0
    qh = q.reshape(B, n_q, MEM_HEADS, HEAD_DIM)
    place = jnp.eye(n_seq, dtype=F32)[jnp.arange(B) % n_seq]
    qbd = jnp.einsum('bthd,hg,bj->bhdjgt', qh, jnp.eye(MEM_HEADS, dtype=F32), place)
    qbd = qbd.reshape(B, MEM_WIDTH, XATTN_LANES).astype(BF16)
    mem = pl.BlockSpec((None, n_seq, mem_rows, HEAD_DIM), lambda b: (layer, b, 0, 0))
    return pl.pallas_call(
        functools.partial(_xattn_packed_kernel, n_q=n_q),
        out_shape=jax.ShapeDtypeStruct((B, n_q, MEM_WIDTH), BF16),
        grid=(B // n_seq,),
        in_specs=[pl.BlockSpec((n_seq, MEM_WIDTH, XATTN_LANES), lambda b: (b, 0, 0)), mem, mem],
        out_specs=pl.BlockSpec((n_seq, n_q, MEM_WIDTH), lambda b: (b, 0, 0)),
        compiler_params=_params("parallel"),
        name="cross_attention_packed",
    )(qbd, mem_k, mem_v)


ROW_TILE = 512
COL_TILE = 512
WIDE_COL_TILE = 1536
LANES = 128


def _wide_col_tile(n):
    return max(t for t in range(LANES, min(n, WIDE_COL_TILE) + 1, LANES) if n % t == 0)


def kernel(x_prompt, x_sample, cache_sb_k, cache_sb_v, state_lru_h, state_lru_conv, state_ret, cache_swa_k, cache_swa_v, cache_mem_k, cache_mem_v, page_table, mem_prompt, ffn1_norm, ffn1_wg, ffn1_wu, ffn1_wd, mix_norm, ab_w_in, ab_w_out, sb_bias, lru_conv_w, lru_conv_b, lru_wa, lru_ba, lru_wx, lru_bx, lru_lambda, cd_w_in, cd_w_out, ret_norm, swa_sinks, xattn_norm, mem_norm, xattn_wq, xattn_wk, xattn_wv, xattn_wo, ffn2_norm, ffn2_wg, ffn2_wu, ffn2_wd, final_norm):
    Bp, Lp, D = x_prompt.shape
    Bs, Ls, _ = x_sample.shape
    assert Bp == 1
    Ms = Bs * Ls
    depth = ffn1_norm.shape[0]
    n_mem = mem_prompt.shape[1]
    past_len = page_table.shape[1] * PAGE_SIZE
    wb = cache_swa_k.shape[1]
    bf = lambda w: w.astype(BF16)
    lru_p = dict(lru_conv_w=lru_conv_w, lru_conv_b=lru_conv_b, lru_wa=lru_wa, lru_wx=lru_wx, lru_ba=lru_ba,
                 lru_bx=lru_bx, lru_lambda=lru_lambda)
    dense_in = lambda x, g, w: norm_matmul(x, g, w, tm=ROW_TILE, tn=_wide_col_tile(w.shape[1]))
    dense_ffn = functools.partial(ffn, tm=ROW_TILE, tf=COL_TILE)
    dense_out = functools.partial(matmul_residual, tm=ROW_TILE, tn=2 * COL_TILE)
    tmajor = lambda t: jnp.transpose(t, (1, 0, 2))

    mem_kv = [norm_matmul(mem_prompt[0], mem_norm[l], bf(jnp.concatenate([xattn_wk[l], xattn_wv[l]], axis=1)),
                          tm=n_mem, tn=COL_TILE) for l in range(depth)]
    mem_k_p = jnp.stack([kv[:, :MEM_WIDTH] for kv in mem_kv])[:, None]
    mem_v_p = jnp.stack([kv[:, MEM_WIDTH:] for kv in mem_kv])[:, None]
    mem_k_s = cache_mem_k.reshape(depth, Bs, n_mem * MEM_HEADS, HEAD_DIM)
    mem_v_s = cache_mem_v.reshape(depth, Bs, n_mem * MEM_HEADS, HEAD_DIM)
    ffn1_w = (bf(ffn1_wg), bf(ffn1_wu), bf(ffn1_wd))
    ffn2_w = (bf(ffn2_wg), bf(ffn2_wu), bf(ffn2_wd))

    x = jnp.concatenate([x_prompt.reshape(Lp, D), x_sample.reshape(Ms, D)], axis=0)
    for layer in range(depth):
        x = dense_ffn(x, ffn1_norm[layer], *ffn1_w, layer)
        if layer % 2 == 0:
            proj = dense_in(x, mix_norm[layer], bf(ab_w_in))
            sproj = proj[Lp:].reshape(Bs, Ls, -1)
            q_s, k_s, v_s = [sproj[..., i * SB_WIDTH:(i + 1) * SB_WIDTH] for i in range(3)]
            a_p = sb_prompt(proj, sb_bias, L=Lp)
            a_s = sb_sample(q_s, k_s, v_s, cache_sb_k, cache_sb_v, page_table, sb_bias)
            b_p, lru_h_p, lru_c_p = lru_prompt(proj, 3, 4, jnp.zeros((CONV_W - 1, LRU_WIDTH), F32),
                                               jnp.zeros((1, LRU_WIDTH), F32), lru_p, L=Lp)
            b_s, lru_h_s, lru_c_s = lru_sample(tmajor(sproj[..., 3 * SB_WIDTH:3 * SB_WIDTH + LRU_WIDTH]),
                                               tmajor(sproj[..., 3 * SB_WIDTH + LRU_WIDTH:]),
                                               tmajor(state_lru_conv), state_lru_h, lru_p)
            x = dense_out(x, [(a_p, a_s.reshape(Ms, SB_WIDTH)), (b_p, tmajor(b_s).reshape(Ms, LRU_WIDTH))],
                          bf(ab_w_out))
            sb_k_p = proj[:Lp, SB_WIDTH:2 * SB_WIDTH].reshape(1, Lp, SB_HEADS, HEAD_DIM)
            sb_v_p = proj[:Lp, 2 * SB_WIDTH:3 * SB_WIDTH].reshape(1, Lp, SB_HEADS, HEAD_DIM)
            sb_k_s = k_s.reshape(Bs, Ls, SB_HEADS, HEAD_DIM)
            sb_v_s = v_s.reshape(Bs, Ls, SB_HEADS, HEAD_DIM)
            lru_c_p = lru_c_p[None]
            lru_c_s = tmajor(lru_c_s)
        else:
            proj = dense_in(x, mix_norm[layer], bf(cd_w_in))
            sproj = proj[Lp:].reshape(Bs, Ls, -1)
            pos_p, pos_s = jnp.arange(Lp), past_len + jnp.arange(Ls)
            c_p, ret_p = retention(proj[None], (0, 1, 2, 3), *rope_tables(pos_p, RET_DK // 2),
                                   jnp.zeros((1, RET_HEADS, RET_DK, RET_DK), F32), ret_norm, L=Lp)
            c_s, ret_s = retention(sproj, (0, 1, 2, 3), *rope_tables(pos_s, RET_DK // 2), state_ret, ret_norm, L=Ls)
            kv0 = 4 * RET_WIDTH + SWA_WIDTH
            d_p, k_rot_p = swa_prompt(proj, 4 * RET_WIDTH // SWA_WIDTH, kv0 // SWA_KV_WIDTH, kv0 // SWA_KV_WIDTH + 1,
                                      *rope_tables_full(pos_p), swa_sinks, L=Lp)
            d_s, k_rot_s = swa_sample(sproj[..., 4 * RET_WIDTH:kv0], sproj[..., kv0:kv0 + SWA_KV_WIDTH],
                                      sproj[..., kv0 + SWA_KV_WIDTH:], cache_swa_k.reshape(Bs, wb, SWA_KV_WIDTH),
                                      cache_swa_v.reshape(Bs, wb, SWA_KV_WIDTH), swa_sinks, past_len=past_len)
            x = dense_out(x, [(c_p[0], c_s.reshape(Ms, RET_WIDTH)), (d_p, d_s.reshape(Ms, SWA_WIDTH))],
                          bf(cd_w_out))
            wp = min(WINDOW, Lp)
            swa_k_p = k_rot_p[Lp - wp:].reshape(1, wp, SWA_KV_HEADS, HEAD_DIM)
            swa_v_p = proj[Lp - wp:Lp, kv0 + SWA_KV_WIDTH:].reshape(1, wp, SWA_KV_HEADS, HEAD_DIM)
            kv_heads = lambda t: t.reshape(Bs, Ls, SWA_KV_HEADS, HEAD_DIM)
            swa_k_s = jnp.concatenate([cache_swa_k, kv_heads(k_rot_s)], axis=1)[:, Ls:]
            swa_v_s = jnp.concatenate([cache_swa_v, kv_heads(sproj[..., kv0 + SWA_KV_WIDTH:])], axis=1)[:, Ls:]
        qx = dense_in(x, xattn_norm[layer], bf(xattn_wq[layer]))
        o_p = cross_attention(qx[None], mem_k_p, mem_v_p, layer, L=Lp, tq=ROW_TILE)
        o_s = cross_attention_packed(qx[Lp:].reshape(Bs, Ls, MEM_WIDTH), mem_k_s, mem_v_s, layer)
        x = dense_out(x, [(o_p[0], o_s.reshape(Ms, MEM_WIDTH))], bf(xattn_wo[layer]))
        x = dense_ffn(x, ffn2_norm[layer], *ffn2_w, layer)
    y = rmsnorm(x, final_norm, tm=ROW_TILE)
    mem_heads = lambda t: t.reshape(depth, 1, n_mem, MEM_HEADS, HEAD_DIM)
    return (y[:Lp].reshape(1, Lp, D), y[Lp:].reshape(Bs, Ls, D), sb_k_p, sb_v_p, sb_k_s, sb_v_s,
            lru_h_p, lru_h_s, lru_c_p, lru_c_s, ret_p, ret_s, swa_k_p, swa_v_p, swa_k_s, swa_v_s,
            mem_heads(mem_k_p), mem_heads(mem_v_p))
```
